```python
import jax, jax.numpy as jnp
from jax import lax
import numpy as np

D_MODEL = 4096
BATCH = 2
SEQ = 8192
DEPTH = 2
DEC_BATCH = 16
DEC_SEQ = 32
PAST_LEN = 2048

CHUNK = 64
Q_BLOCK = 128
HEAD_DIM = 128
N_HEADS_TOTAL = D_MODEL // HEAD_DIM
H_MLA = N_HEADS_TOTAL // 4
H_FOX = (N_HEADS_TOTAL - H_MLA) // 2
H_SB = N_HEADS_TOTAL - H_MLA - H_FOX
MLA_NOPE = 128
MLA_ROPE = 64
MLA_V = 128
Q_LORA = D_MODEL // 4
KV_LORA = 512
ROPE_THETA = 10000.0
D_FF = 256 * ((8 * D_MODEL // 3 + 255) // 256)
N_MOD = 9
EPS = 1e-6
W_FOX = H_FOX * HEAD_DIM
W_SB = H_SB * HEAD_DIM
W_MLA = H_MLA * MLA_V
MIX_WIDTH = W_FOX + W_SB + W_MLA
IN_WIDTHS = (W_FOX, W_FOX, W_FOX, H_FOX, W_SB, W_SB, W_SB, Q_LORA, KV_LORA, MLA_ROPE)
IN_WIDTH = sum(IN_WIDTHS)
IN_SPLITS = tuple(int(v) for v in np.cumsum(IN_WIDTHS)[:-1])
MIX_SPLITS = (W_FOX, W_FOX + W_SB)

kernel_name = 'hybrid_streaming_encoder_step'


def rmsnorm(x, g):
    xf = x.astype(jnp.float32)
    y = xf * lax.rsqrt(jnp.mean(xf * xf, axis=-1, keepdims=True) + EPS)
    return (y * g.astype(jnp.float32)).astype(x.dtype)


def swiglu(h, wg, wu, wd):
    return (jax.nn.silu(h @ wg) * (h @ wu)) @ wd


def rope_tables(pos):
    half = MLA_ROPE // 2
    inv = ROPE_THETA ** (-jnp.arange(half, dtype=jnp.float32) / half)
    ang = pos.astype(jnp.float32)[:, None] * inv[None, :]
    return jnp.cos(ang), jnp.sin(ang)


def apply_rope(x, cos, sin):
    half = x.shape[-1] // 2
    xf = x.astype(jnp.float32)
    x1, x2 = xf[..., :half], xf[..., half:]
    return jnp.concatenate([x1 * cos - x2 * sin, x1 * sin + x2 * cos], axis=-1).astype(x.dtype)


def fox_block(q, Fq, qpos, k, v, Fk, kpos):
    s = jnp.einsum('bqhd,bkhd->bhqk', q, k, preferred_element_type=jnp.float32) * (HEAD_DIM ** -0.5)
    s = s + (Fq.transpose(0, 2, 1)[:, :, :, None] - Fk.transpose(0, 2, 1)[:, :, None, :])
    s = jnp.where(kpos[None, :] <= qpos[:, None], s, -jnp.inf)
    p = jax.nn.softmax(s, axis=-1)
    return jnp.einsum('bhqk,bkhd->bqhd', p.astype(v.dtype), v)


def sb_block(q, qpos, k, v, kpos):
    z = jnp.einsum('bqhd,bkhd->bhqk', q, k, preferred_element_type=jnp.float32) * (HEAD_DIM ** -0.5)
    mask = kpos[None, :] < qpos[:, None]
    log_1mb = jnp.where(mask, jax.nn.log_sigmoid(-z), 0.0)
    between = lax.cumsum(log_1mb, axis=3, reverse=True) - log_1mb
    a = jnp.where(mask, jnp.exp(jax.nn.log_sigmoid(z) + between), 0.0)
    return jnp.einsum('bhqk,bkhd->bqhd', a.astype(v.dtype), v)


def mla_block(q, qpos, k, v, kpos):
    s = jnp.einsum('bqhd,bkhd->bhqk', q, k, preferred_element_type=jnp.float32) * ((MLA_NOPE + MLA_ROPE) ** -0.5)
    s = jnp.where((kpos[None, :] // CHUNK) <= (qpos[:, None] // CHUNK), s, -jnp.inf)
    p = jax.nn.softmax(s, axis=-1)
    return jnp.einsum('bhqk,bkhd->bqhd', p.astype(v.dtype), v)


def attend(fn, q_args, qpos, kv_args):
    B, S = q_args[0].shape[:2]
    if S <= Q_BLOCK:
        return fn(*q_args, qpos, *kv_args)
    n = S // Q_BLOCK
    blk = lambda a: a.reshape((B, n, Q_BLOCK) + a.shape[2:]).swapaxes(0, 1)
    xs = (tuple(blk(a) for a in q_args), qpos.reshape(n, Q_BLOCK))
    out = lax.map(lambda t: fn(*t[0], t[1], *kv_args), xs)
    return out.swapaxes(0, 1).reshape((B, S) + out.shape[3:])


def mla_expand(ckv, krope, w_kv_up):
    B, K, _ = ckv.shape
    kv = (ckv @ w_kv_up).reshape(B, K, H_MLA, MLA_NOPE + MLA_V)
    k_nope, v = kv[..., :MLA_NOPE], kv[..., MLA_NOPE:]
    k_r = jnp.broadcast_to(krope[:, :, None, :], (B, K, H_MLA, MLA_ROPE)).astype(k_nope.dtype)
    return jnp.concatenate([k_nope, k_r], axis=-1), v


def token_mix(h, pos, lp, cache):
    B, S, _ = h.shape
    proj = h @ lp['w_in']
    fq, fk, fv, ff, sq, sk, sv, mq, mkv, mkr = jnp.split(proj, IN_SPLITS, axis=-1)
    fq, fk, fv = [a.reshape(B, S, H_FOX, HEAD_DIM) for a in (fq, fk, fv)]
    sq, sk, sv = [a.reshape(B, S, H_SB, HEAD_DIM) for a in (sq, sk, sv)]
    logf = jax.nn.log_sigmoid((ff + lp['b_forget']).astype(jnp.float32))
    cos, sin = rope_tables(pos)
    mla_q = (rmsnorm(mq, lp['g_q_lat']) @ lp['w_q_up']).reshape(B, S, H_MLA, MLA_NOPE + MLA_ROPE)
    mla_q = jnp.concatenate([mla_q[..., :MLA_NOPE], apply_rope(mla_q[..., MLA_NOPE:], cos[:, None, :], sin[:, None, :])], axis=-1)
    ckv = rmsnorm(mkv, lp['g_kv_lat'])
    krope = apply_rope(mkr, cos, sin)
    new_state = (fk, fv, logf, sk, sv, ckv, krope)
    if cache is None:
        keys = new_state
        kpos = pos
    else:
        keys = tuple(jnp.concatenate([cc, nn.astype(cc.dtype)], axis=1) for cc, nn in zip(cache, new_state))
        kpos = jnp.arange(keys[0].shape[1])
    k_fox, v_fox, logf_all, k_sb, v_sb, ckv_all, krope_all = keys
    F = lax.cumsum(logf_all.astype(jnp.float32), axis=1)
    mla_k, mla_v = mla_expand(ckv_all, krope_all, lp['w_kv_up'])
    o_fox = attend(fox_block, (fq, F[:, -S:]), pos, (k_fox, v_fox, F, kpos))
    o_sb = attend(sb_block, (sq,), pos, (k_sb, v_sb, kpos))
    o_mla = attend(mla_block, (mla_q,), pos, (mla_k, mla_v, kpos))
    gains = jnp.split(lp['g_mix_out'], MIX_SPLITS)
    groups = [rmsnorm(o.reshape(B, S, -1), g) for o, g in zip((o_fox, o_sb, o_mla), gains)]
    return jnp.concatenate(groups, axis=-1) @ lp['w_out'], new_state


def trunk_layer(x, c, pos, lp, cache):
    mod = (jax.nn.silu(c) @ lp['w_ada'] + lp['b_ada']).reshape(c.shape[0], N_MOD, D_MODEL)
    m = [mod[:, i, None, :] for i in range(N_MOD)]
    h = rmsnorm(x, lp['g_norm'][0]) * (1 + m[1]) + m[0]
    x = x + 0.5 * m[2] * swiglu(h, lp['w_ffn_gate'][0], lp['w_ffn_up'][0], lp['w_ffn_down'][0])
    h = rmsnorm(x, lp['g_norm'][1]) * (1 + m[4]) + m[3]
    mix, new_state = token_mix(h, pos, lp, cache)
    x = x + m[5] * mix
    h = rmsnorm(x, lp['g_norm'][2]) * (1 + m[7]) + m[6]
    x = x + 0.5 * m[8] * swiglu(h, lp['w_ffn_gate'][1], lp['w_ffn_up'][1], lp['w_ffn_down'][1])
    return x, new_state


def setup_inputs(seed: int = 0) -> dict:
    key = jax.random.key(seed)
    ks = jax.random.split(key, 32)
    nrm = lambda k, shape, s=1.0: s * jax.random.normal(k, shape, dtype=jnp.float32)
    cshape = (DEPTH, DEC_BATCH, PAST_LEN)
    return {
        'x_prompt': nrm(ks[0], (BATCH, SEQ, D_MODEL)),
        'x_sample': nrm(ks[1], (DEC_BATCH, DEC_SEQ, D_MODEL)),
        'c_prompt': nrm(ks[2], (BATCH, D_MODEL)),
        'c_sample': nrm(ks[3], (DEC_BATCH, D_MODEL)),
        'cache_fox_k': nrm(ks[4], cshape + (H_FOX, HEAD_DIM)),
        'cache_fox_v': nrm(ks[5], cshape + (H_FOX, HEAD_DIM)),
        'cache_fox_logf': jax.nn.log_sigmoid(3.0 + nrm(ks[6], cshape + (H_FOX,), 0.5)),
        'cache_sb_k': nrm(ks[7], cshape + (H_SB, HEAD_DIM)),
        'cache_sb_v': nrm(ks[8], cshape + (H_SB, HEAD_DIM)),
        'cache_mla_ckv': nrm(ks[9], cshape + (KV_LORA,)),
        'cache_mla_krope': nrm(ks[10], cshape + (MLA_ROPE,)),
        'w_ada': nrm(ks[11], (DEPTH, D_MODEL, N_MOD * D_MODEL), 0.5 * D_MODEL ** -0.5),
        'b_ada': nrm(ks[12], (DEPTH, N_MOD * D_MODEL), 0.01),
        'g_norm': 1.0 + nrm(ks[13], (DEPTH, 3, D_MODEL), 0.02),
        'w_in': nrm(ks[14], (DEPTH, D_MODEL, IN_WIDTH), D_MODEL ** -0.5),
        'b_forget': 3.0 + nrm(ks[15], (DEPTH, H_FOX), 0.5),
        'g_q_lat': 1.0 + nrm(ks[16], (DEPTH, Q_LORA), 0.02),
        'w_q_up': nrm(ks[17], (DEPTH, Q_LORA, H_MLA * (MLA_NOPE + MLA_ROPE)), Q_LORA ** -0.5),
        'g_kv_lat': 1.0 + nrm(ks[18], (DEPTH, KV_LORA), 0.02),
        'w_kv_up': nrm(ks[19], (DEPTH, KV_LORA, H_MLA * (MLA_NOPE + MLA_V)), KV_LORA ** -0.5),
        'g_mix_out': 1.0 + nrm(ks[20], (DEPTH, MIX_WIDTH), 0.02),
        'w_out': nrm(ks[21], (DEPTH, MIX_WIDTH, D_MODEL), MIX_WIDTH ** -0.5),
        'w_ffn_gate': nrm(ks[22], (DEPTH, 2, D_MODEL, D_FF), D_MODEL ** -0.5),
        'w_ffn_up': nrm(ks[23], (DEPTH, 2, D_MODEL, D_FF), D_MODEL ** -0.5),
        'w_ffn_down': nrm(ks[24], (DEPTH, 2, D_FF, D_MODEL), D_FF ** -0.5),
        'g_final': 1.0 + nrm(ks[25], (D_MODEL,), 0.02),
    }


def reference(x_prompt, x_sample, c_prompt, c_sample, cache_fox_k, cache_fox_v, cache_fox_logf, cache_sb_k, cache_sb_v, cache_mla_ckv, cache_mla_krope, w_ada, b_ada, g_norm, w_in, b_forget, g_q_lat, w_q_up, g_kv_lat, w_kv_up, g_mix_out, w_out, w_ffn_gate, w_ffn_up, w_ffn_down, g_final):
    pos_p = jnp.arange(x_prompt.shape[1])
    pos_s = cache_fox_k.shape[2] + jnp.arange(x_sample.shape[1])
    caches = (cache_fox_k, cache_fox_v, cache_fox_logf, cache_sb_k, cache_sb_v, cache_mla_ckv, cache_mla_krope)
    xp, xs = x_prompt, x_sample
    new_p, new_s = [], []
    for l in range(DEPTH):
        lp = {'w_ada': w_ada[l], 'b_ada': b_ada[l], 'g_norm': g_norm[l], 'w_in': w_in[l], 'b_forget': b_forget[l],
              'g_q_lat': g_q_lat[l], 'w_q_up': w_q_up[l], 'g_kv_lat': g_kv_lat[l], 'w_kv_up': w_kv_up[l],
              'g_mix_out': g_mix_out[l], 'w_out': w_out[l], 'w_ffn_gate': w_ffn_gate[l], 'w_ffn_up': w_ffn_up[l],
              'w_ffn_down': w_ffn_down[l]}
        xp, st_p = trunk_layer(xp, c_prompt, pos_p, lp, None)
        xs, st_s = trunk_layer(xs, c_sample, pos_s, lp, tuple(cc[l] for cc in caches))
        new_p.append(st_p)
        new_s.append(st_s)
    y_prompt = rmsnorm(xp, g_final)
    y_sample = rmsnorm(xs, g_final)
    stack = lambda states, i: jnp.stack([st[i] for st in states])
    p_fox_k, p_fox_v, p_fox_logf = stack(new_p, 0), stack(new_p, 1), stack(new_p, 2)
    p_sb_k, p_sb_v = stack(new_p, 3), stack(new_p, 4)
    p_mla_ckv, p_mla_krope = stack(new_p, 5), stack(new_p, 6)
    s_fox_k, s_fox_v, s_fox_logf = stack(new_s, 0), stack(new_s, 1), stack(new_s, 2)
    s_sb_k, s_sb_v = stack(new_s, 3), stack(new_s, 4)
    s_mla_ckv, s_mla_krope = stack(new_s, 5), stack(new_s, 6)
    return (y_prompt, y_sample, p_fox_k, p_fox_v, p_fox_logf, p_sb_k, p_sb_v, p_mla_ckv, p_mla_krope, s_fox_k, s_fox_v, s_fox_logf, s_sb_k, s_sb_v, s_mla_ckv, s_mla_krope)
```

```python
import functools

import jax
import jax.numpy as jnp
from jax import lax
from jax.experimental import pallas as pl
from jax.experimental.pallas import tpu as pltpu

F32 = jnp.float32
BF16 = jnp.bfloat16

HEAD_DIM = 128
MLA_NOPE = 128
MLA_V = 128
CHUNK = 64
N_MOD = 9
EPS = 1e-6
ROPE_THETA = 10000.0

LANES = 128
SUBLANES = 8
MLA_QK_PAD = 256
VMEM_LIMIT = 56 * 1024 * 1024


def _cparams(n_axes, sequential_axes=()):
    sem = tuple("arbitrary" if a in sequential_axes else "parallel" for a in range(n_axes))
    return pltpu.CompilerParams(dimension_semantics=sem, vmem_limit_bytes=VMEM_LIMIT)


def _tile(n, pref, align):
    if n <= pref:
        return n
    for t in range(pref - pref % align, 0, -align):
        if n % t == 0:
            return t
    return n


def _dot(a, b):
    return jnp.dot(a, b, preferred_element_type=F32)


def _dot_nt(a, b):
    return lax.dot_general(a, b, (((1,), (1,)), ((), ())), preferred_element_type=F32)


def _softplus_neg_abs(z):
    return jnp.log1p(jnp.exp(-jnp.abs(z)))


def _rms(x):
    return x * lax.rsqrt(jnp.mean(x * x, axis=-1, keepdims=True) + EPS)


def _rope_tile(t, cos, sin):
    lane = lax.broadcasted_iota(jnp.int32, t.shape, 1)
    half = LANES // 4
    partner = jnp.where(lane < half, pltpu.roll(t, LANES - half, 1), pltpu.roll(t, half, 1))
    return t * cos + partner * sin


def _ada_kernel(c_ref, w_ref, b_ref, o_ref):
    c = c_ref[...]
    a = (c / (1.0 + jnp.exp(-c))).astype(BF16)
    o_ref[0] = _dot(a, w_ref[0].astype(BF16)) + b_ref[0]


def _ada(c_all, w_ada, b_ada):
    depth, d, n = w_ada.shape
    nb = c_all.shape[0]
    tn = _tile(n, 512, LANES)
    return pl.pallas_call(
        _ada_kernel,
        grid=(depth, n // tn),
        in_specs=[
            pl.BlockSpec((nb, d), lambda l, j: (0, 0)),
            pl.BlockSpec((1, d, tn), lambda l, j: (l, 0, j)),
            pl.BlockSpec((1, 1, tn), lambda l, j: (l, 0, j)),
        ],
        out_specs=pl.BlockSpec((1, nb, tn), lambda l, j: (l, 0, j)),
        out_shape=jax.ShapeDtypeStruct((depth, nb, n), F32),
        compiler_params=_cparams(2),
        name="ada",
    )(c_all, w_ada, b_ada.reshape(depth, 1, n))


def _prenorm_kernel(*refs, coef, has_res, final):
    refs = list(refs)
    x_ref = refs.pop(0)
    x = x_ref[0]
    if has_res:
        y_ref = refs.pop(0)
        gate_ref = refs.pop(0)
        x = x + (coef * gate_ref[0]) * y_ref[0]
    g_ref = refs.pop(0)
    if final:
        (o_ref,) = refs
        o_ref[0] = _rms(x) * g_ref[...]
        return
    sc_ref = refs.pop(0)
    sh_ref = refs.pop(0)
    if has_res:
        xo_ref = refs.pop(0)
        xo_ref[0] = x
    (h_ref,) = refs
    h = (_rms(x) * g_ref[...]) * (1.0 + sc_ref[0]) + sh_ref[0]
    h_ref[0] = h.astype(BF16)


def _prenorm(x, g, scale=None, shift=None, res=None, final=False):
    b, s, d = x.shape
    tr = _tile(s, 256, SUBLANES)
    row = pl.BlockSpec((1, tr, d), lambda bi, i: (bi, i, 0))
    per_b = pl.BlockSpec((1, 1, d), lambda bi, i: (bi, 0, 0))
    shared = pl.BlockSpec((1, d), lambda bi, i: (0, 0))
    args, specs = [x], [row]
    coef = 0.0
    if res is not None:
        y, gate, coef = res
        args += [y, gate]
        specs += [row, per_b]
    args.append(g.reshape(1, d))
    specs.append(shared)
    out_shape, out_specs = [], []
    if final:
        out_shape.append(jax.ShapeDtypeStruct((b, s, d), F32))
        out_specs.append(row)
    else:
        args += [scale, shift]
        specs += [per_b, per_b]
        if res is not None:
            out_shape.append(jax.ShapeDtypeStruct((b, s, d), F32))
            out_specs.append(row)
        out_shape.append(jax.ShapeDtypeStruct((b, s, d), BF16))
        out_specs.append(row)
    outs = pl.pallas_call(
        functools.partial(_prenorm_kernel, coef=coef, has_res=res is not None, final=final),
        grid=(b, s // tr),
        in_specs=specs,
        out_specs=out_specs,
        out_shape=out_shape,
        compiler_params=_cparams(2),
        name="final_norm" if final else "prenorm",
    )(*args)
    if final:
        return outs[0]
    if res is None:
        return x, outs[0]
    return outs[0], outs[1]


def _mm_kernel(a_ref, w_ref, *o_refs):
    acc = _dot(a_ref[...].astype(BF16), w_ref[...])
    for o_ref in o_refs:
        o_ref[...] = acc.astype(o_ref.dtype)


def _mm(a, w, out_dtypes, name, tm_pref=512, tn_pref=256):
    m, k = a.shape
    n = w.shape[1]
    tm = _tile(m, tm_pref, SUBLANES)
    tn = _tile(n, tn_pref, LANES)
    outs = pl.pallas_call(
        _mm_kernel,
        grid=(m // tm, n // tn),
        in_specs=[
            pl.BlockSpec((tm, k), lambda i, j: (i, 0)),
            pl.BlockSpec((k, tn), lambda i, j: (0, j)),
        ],
        out_specs=[pl.BlockSpec((tm, tn), lambda i, j: (i, j)) for _ in out_dtypes],
        out_shape=[jax.ShapeDtypeStruct((m, n), dt) for dt in out_dtypes],
        compiler_params=_cparams(2),
        name=name,
    )(a, w)
    return outs


def _swiglu_kernel(a_ref, wg_ref, wu_ref, o_ref):
    a = a_ref[...]
    g = _dot(a, wg_ref[...])
    u = _dot(a, wu_ref[...])
    o_ref[...] = ((g / (1.0 + jnp.exp(-g))) * u).astype(o_ref.dtype)


def _swiglu(a, wg, wu, tm_pref=1024, tn_pref=256):
    m, k = a.shape
    n = wg.shape[1]
    tm = _tile(m, tm_pref, SUBLANES)
    tn = _tile(n, tn_pref, LANES)
    wspec = pl.BlockSpec((k, tn), lambda i, j: (0, j))
    return pl.pallas_call(
        _swiglu_kernel,
        grid=(m // tm, n // tn),
        in_specs=[pl.BlockSpec((tm, k), lambda i, j: (i, 0)), wspec, wspec],
        out_specs=pl.BlockSpec((tm, tn), lambda i, j: (i, j)),
        out_shape=jax.ShapeDtypeStruct((m, n), BF16),
        compiler_params=_cparams(2),
        name="ffn_swiglu",
    )(a, wg, wu)


def _mla_prep_kernel(m_ref, gq_ref, gkv_ref, bf_ref, cos_ref, sin_ref,
                     ql_ref, ckv32_ref, ckv16_ref, kr_ref, *, q_lora, kv_lora, n_forget):
    x = m_ref[...]
    ql_ref[...] = (_rms(x[:, :q_lora]) * gq_ref[...]).astype(BF16)
    ckv = _rms(x[:, q_lora:q_lora + kv_lora]) * gkv_ref[...]
    ckv32_ref[...] = ckv
    ckv16_ref[...] = ckv.astype(BF16)
    t = x[:, q_lora + kv_lora:]
    kr = _rope_tile(t, cos_ref[...], sin_ref[...])
    z = t + bf_ref[...]
    logf = jnp.minimum(z, 0.0) - _softplus_neg_abs(z)
    lane = lax.broadcasted_iota(jnp.int32, t.shape, 1)
    rope_w = LANES // 2
    kr_ref[...] = jnp.where(lane < rope_w, kr, jnp.where(lane < rope_w + n_forget, logf, 0.0))


def _mla_prep(mlat, gq, gkv, bf_tile, cos_t, sin_t, q_lora, kv_lora, n_forget):
    m = mlat.shape[0]
    s = cos_t.shape[0]
    tr = _tile(s, 256, SUBLANES)
    ns = s // tr
    rows = lambda w: pl.BlockSpec((tr, w), lambda i: (i, 0))
    shared = lambda w: pl.BlockSpec((1, w), lambda i: (0, 0))
    table = pl.BlockSpec((tr, LANES), lambda i: (i % ns, 0))
    return pl.pallas_call(
        functools.partial(_mla_prep_kernel, q_lora=q_lora, kv_lora=kv_lora, n_forget=n_forget),
        grid=(m // tr,),
        in_specs=[rows(mlat.shape[1]), shared(q_lora), shared(kv_lora), shared(LANES), table, table],
        out_specs=[rows(q_lora), rows(kv_lora), rows(kv_lora), rows(LANES)],
        out_shape=[
            jax.ShapeDtypeStruct((m, q_lora), BF16),
            jax.ShapeDtypeStruct((m, kv_lora), F32),
            jax.ShapeDtypeStruct((m, kv_lora), BF16),
            jax.ShapeDtypeStruct((m, LANES), F32),
        ],
        compiler_params=_cparams(1),
        name="mla_prep",
    )(mlat, gq.reshape(1, q_lora), gkv.reshape(1, kv_lora), bf_tile, cos_t, sin_t)


def _q_up_kernel(a_ref, w_ref, cos_ref, sin_ref, o_ref, *, n_heads):
    acc = _dot(a_ref[...], w_ref[...])
    cos = cos_ref[...]
    sin = sin_ref[...]
    for h in range(n_heads):
        base = h * MLA_QK_PAD
        o_ref[:, base:base + MLA_NOPE] = acc[:, base:base + MLA_NOPE].astype(BF16)
        t = acc[:, base + MLA_NOPE:base + MLA_QK_PAD]
        o_ref[:, base + MLA_NOPE:base + MLA_QK_PAD] = _rope_tile(t, cos, sin).astype(BF16)


def _q_up(qlat, wq, cos_t, sin_t, n_heads):
    m, k = qlat.shape
    n = wq.shape[1]
    s = cos_t.shape[0]
    tr = _tile(s, 512, SUBLANES)
    ns = s // tr
    table = pl.BlockSpec((tr, LANES), lambda i: (i % ns, 0))
    return pl.pallas_call(
        functools.partial(_q_up_kernel, n_heads=n_heads),
        grid=(m // tr,),
        in_specs=[pl.BlockSpec((tr, k), lambda i: (i, 0)), pl.BlockSpec((k, n), lambda i: (0, 0)), table, table],
        out_specs=pl.BlockSpec((tr, n), lambda i: (i, 0)),
        out_shape=jax.ShapeDtypeStruct((m, n), BF16),
        compiler_params=_cparams(1),
        name="mla_q_up",
    )(qlat, wq, cos_t, sin_t)


def _kv_up_kernel(a_ref, wk_ref, wv_ref, kr_ref, k_ref, v_ref, *, n_heads):
    a = a_ref[...].astype(BF16)
    kn = _dot(a, wk_ref[...])
    v_ref[...] = _dot(a, wv_ref[...]).astype(BF16)
    kr = kr_ref[...]
    lane = lax.broadcasted_iota(jnp.int32, kr.shape, 1)
    krm = jnp.where(lane < LANES // 2, kr, 0.0).astype(BF16)
    for h in range(n_heads):
        base = h * MLA_QK_PAD
        k_ref[:, base:base + MLA_NOPE] = kn[:, h * MLA_NOPE:(h + 1) * MLA_NOPE].astype(BF16)
        k_ref[:, base + MLA_NOPE:base + MLA_QK_PAD] = krm


def _kv_up(ckv, wk, wv, kr, n_heads):
    m, k = ckv.shape
    tm = _tile(m, 512, SUBLANES)
    rows = lambda w: pl.BlockSpec((tm, w), lambda i: (i, 0))
    whole = lambda a: pl.BlockSpec(a.shape, lambda i: (0, 0))
    return pl.pallas_call(
        functools.partial(_kv_up_kernel, n_heads=n_heads),
        grid=(m // tm,),
        in_specs=[rows(k), whole(wk), whole(wv), rows(LANES)],
        out_specs=[rows(n_heads * MLA_QK_PAD), rows(n_heads * MLA_V)],
        out_shape=[
            jax.ShapeDtypeStruct((m, n_heads * MLA_QK_PAD), BF16),
            jax.ShapeDtypeStruct((m, n_heads * MLA_V), BF16),
        ],
        compiler_params=_cparams(1),
        name="mla_kv_up",
    )(ckv, wk, wv, kr)


def _split3(x):
    x1 = x.astype(BF16)
    r1 = x - x1.astype(F32)
    x2 = r1.astype(BF16)
    x3 = (r1 - x2.astype(F32)).astype(BF16)
    return x1, x2, x3


def _cumsum_kernel(x_ref, o_ref, carry_ref, *, tc):
    @pl.when(pl.program_id(1) == 0)
    def _():
        carry_ref[...] = jnp.zeros_like(carry_ref)

    r = lax.broadcasted_iota(jnp.int32, (tc, tc), 0)
    c = lax.broadcasted_iota(jnp.int32, (tc, tc), 1)
    tri = jnp.where(c <= r, 1.0, 0.0).astype(BF16)
    x1, x2, x3 = _split3(x_ref[0])
    y = (_dot(tri, x3) + _dot(tri, x2)) + _dot(tri, x1) + carry_ref[...]
    o_ref[0] = y
    carry_ref[...] = y[tc - 1:tc, :]


def _cumsum_rows(x):
    b, s, w = x.shape
    tc = _tile(s, 256, SUBLANES)
    blk = pl.BlockSpec((1, tc, w), lambda bi, j: (bi, j, 0))
    return pl.pallas_call(
        functools.partial(_cumsum_kernel, tc=tc),
        grid=(b, s // tc),
        in_specs=[blk],
        out_specs=blk,
        out_shape=jax.ShapeDtypeStruct((b, s, w), F32),
        scratch_shapes=[pltpu.VMEM((1, w), F32)],
        compiler_params=_cparams(2, sequential_axes=(1,)),
        name="cumsum_rows",
    )(x)


def _causal_mask(shape, chunk):
    r = lax.broadcasted_iota(jnp.int32, shape, 0)
    c = lax.broadcasted_iota(jnp.int32, shape, 1)
    if chunk:
        return (c // chunk) <= (r // chunk)
    return c <= r


def _online_softmax_step(s, v, m_ref, l_ref, acc_ref):
    m_prev = m_ref[...]
    m_new = jnp.maximum(m_prev, jnp.max(s, axis=-1, keepdims=True))
    alpha = jnp.exp(m_prev - m_new)
    p = jnp.exp(s - m_new)
    l_ref[...] = alpha * l_ref[...] + jnp.sum(p, axis=-1, keepdims=True)
    acc_ref[...] = alpha * acc_ref[...] + _dot(p.astype(BF16), v)
    m_ref[...] = m_new


def _softmax_attn_kernel(*refs, t, scale, has_bias, chunk):
    if has_bias:
        q_ref, k_ref, v_ref, fq_ref, fk_ref, o_ref, m_ref, l_ref, acc_ref = refs
    else:
        q_ref, k_ref, v_ref, o_ref, m_ref, l_ref, acc_ref = refs
    i = pl.program_id(2)
    q = q_ref[0]
    m_ref[...] = jnp.full_like(m_ref, -jnp.inf)
    l_ref[...] = jnp.zeros_like(l_ref)
    acc_ref[...] = jnp.zeros_like(acc_ref)

    def step(j, masked):
        ks = pl.multiple_of(j * t, t)
        s = _dot_nt(q, k_ref[0, pl.ds(ks, t), :]) * scale
        if has_bias:
            s = s + (fq_ref[0, 0] - fk_ref[0, 0, pl.ds(j, 1), :])
        if masked:
            s = jnp.where(_causal_mask(s.shape, chunk), s, -jnp.inf)
        _online_softmax_step(s, v_ref[0, pl.ds(ks, t), :], m_ref, l_ref, acc_ref)

    def body(j, carry):
        step(j, False)
        return carry

    lax.fori_loop(0, i, body, 0)
    step(i, True)
    o_ref[0] = acc_ref[...] / l_ref[...]


def _softmax_attn(q, k, v, n_heads, dq, dv, scale, chunk, f_col=None, f_row=None):
    b, s, _ = q.shape
    t = _tile(s, 256, LANES)
    has_bias = f_col is not None
    in_specs = [
        pl.BlockSpec((1, t, dq), lambda bi, h, i: (bi, i, h)),
        pl.BlockSpec((1, s, dq), lambda bi, h, i: (bi, 0, h)),
        pl.BlockSpec((1, s, dv), lambda bi, h, i: (bi, 0, h)),
    ]
    args = [q, k, v]
    if has_bias:
        in_specs += [
            pl.BlockSpec((1, 1, t, 1), lambda bi, h, i: (bi, h, i, 0)),
            pl.BlockSpec((1, 1, s // t, t), lambda bi, h, i: (bi, h, 0, 0)),
        ]
        args += [f_col, f_row.reshape(b, n_heads, s // t, t)]
    return pl.pallas_call(
        functools.partial(_softmax_attn_kernel, t=t, scale=scale, has_bias=has_bias, chunk=chunk),
        grid=(b, n_heads, s // t),
        in_specs=in_specs,
        out_specs=pl.BlockSpec((1, t, dv), lambda bi, h, i: (bi, i, h)),
        out_shape=jax.ShapeDtypeStruct((b, s, n_heads * dv), F32),
        scratch_shapes=[pltpu.VMEM((t, 1), F32), pltpu.VMEM((t, 1), F32), pltpu.VMEM((t, dv), F32)],
        compiler_params=_cparams(3),
        name="fox_attn" if has_bias else "mla_attn",
    )(*args)


def _softmax_attn_cached_kernel(*refs, scale, has_bias, chunk, past):
    if has_bias:
        q_ref, kc_ref, vc_ref, kn_ref, vn_ref, fq_ref, fkc_ref, fkn_ref, o_ref = refs
    else:
        q_ref, kc_ref, vc_ref, kn_ref, vn_ref, o_ref = refs
    q = q_ref[0]
    s_c = _dot_nt(q, kc_ref[0].astype(BF16)) * scale
    s_n = _dot_nt(q, kn_ref[0]) * scale
    if has_bias:
        fq = fq_ref[0, 0]
        s_c = s_c + (fq - fkc_ref[0, 0])
        s_n = s_n + (fq - fkn_ref[0, 0])
    r = lax.broadcasted_iota(jnp.int32, s_n.shape, 0) + past
    c = lax.broadcasted_iota(jnp.int32, s_n.shape, 1) + past
    visible = (c // chunk) <= (r // chunk) if chunk else c <= r
    s_n = jnp.where(visible, s_n, -jnp.inf)
    m = jnp.maximum(jnp.max(s_c, axis=-1, keepdims=True), jnp.max(s_n, axis=-1, keepdims=True))
    p_c = jnp.exp(s_c - m)
    p_n = jnp.exp(s_n - m)
    l = jnp.sum(p_c, axis=-1, keepdims=True) + jnp.sum(p_n, axis=-1, keepdims=True)
    acc = _dot(p_c.astype(BF16), vc_ref[0].astype(BF16)) + _dot(p_n.astype(BF16), vn_ref[0])
    o_ref[0] = acc / l


def _softmax_attn_cached(q, kc, vc, kn, vn, n_heads, dq, dv, scale, chunk, f_col=None, f_row_c=None, f_row_n=None):
    b, sn, _ = q.shape
    past = kc.shape[1]
    has_bias = f_col is not None
    new = lambda d: pl.BlockSpec((1, sn, d), lambda bi, h: (bi, 0, h))
    old = lambda d: pl.BlockSpec((1, past, d), lambda bi, h: (bi, 0, h))
    in_specs = [new(dq), old(dq), old(dv), new(dq), new(dv)]
    args = [q, kc, vc, kn, vn]
    if has_bias:
        in_specs += [
            pl.BlockSpec((1, 1, sn, 1), lambda bi, h: (bi, h, 0, 0)),
            pl.BlockSpec((1, 1, 1, past), lambda bi, h: (bi, h, 0, 0)),
            pl.BlockSpec((1, 1, 1, sn), lambda bi, h: (bi, h, 0, 0)),
        ]
        args += [f_col, f_row_c, f_row_n]
    return pl.pallas_call(
        functools.partial(_softmax_attn_cached_kernel, scale=scale, has_bias=has_bias, chunk=chunk, past=past),
        grid=(b, n_heads),
        in_specs=in_specs,
        out_specs=new(dv),
        out_shape=jax.ShapeDtypeStruct((b, sn, n_heads * dv), F32),
        compiler_params=_cparams(2),
        name="fox_attn_cached" if has_bias else "mla_attn_cached",
    )(*args)


def _suffix_matrix(t):
    r = lax.broadcasted_iota(jnp.int32, (t, t), 0)
    c = lax.broadcasted_iota(jnp.int32, (t, t), 1)
    return jnp.where(r > c, 1.0, 0.0).astype(BF16)


def _sb_step(q, k, v, u, c_ref, acc_ref, scale, masked):
    z = _dot_nt(q, k) * scale
    tail = _softplus_neg_abs(z)
    log_beta = jnp.minimum(z, 0.0) - tail
    log_1mb = -jnp.maximum(z, 0.0) - tail
    if masked:
        r = lax.broadcasted_iota(jnp.int32, z.shape, 0)
        c = lax.broadcasted_iota(jnp.int32, z.shape, 1)
        strict = c < r
        log_1mb = jnp.where(strict, log_1mb, 0.0)
    hi = log_1mb.astype(BF16)
    lo = (log_1mb - hi.astype(F32)).astype(BF16)
    between = _dot(lo, u) + _dot(hi, u) + c_ref[...]
    a = jnp.exp(log_beta + between)
    if masked:
        a = jnp.where(strict, a, 0.0)
    acc_ref[...] += _dot(a.astype(BF16), v)
    c_ref[...] += jnp.sum(log_1mb, axis=-1, keepdims=True)


def _sb_attn_kernel(q_ref, k_ref, v_ref, o_ref, c_ref, acc_ref, *, t, scale):
    i = pl.program_id(2)
    q = q_ref[0]
    u = _suffix_matrix(t)
    c_ref[...] = jnp.zeros_like(c_ref)
    acc_ref[...] = jnp.zeros_like(acc_ref)

    def step(j, masked):
        ks = pl.multiple_of(j * t, t)
        _sb_step(q, k_ref[0, pl.ds(ks, t), :], v_ref[0, pl.ds(ks, t), :], u, c_ref, acc_ref, scale, masked)

    step(i, True)

    def body(jj, carry):
        step(i - 1 - jj, False)
        return carry

    lax.fori_loop(0, i, body, 0)
    o_ref[0] = acc_ref[...]


def _sb_attn(q, k, v, n_heads, scale):
    b, s, _ = q.shape
    d = HEAD_DIM
    t = _tile(s, 256, LANES)
    return pl.pallas_call(
        functools.partial(_sb_attn_kernel, t=t, scale=scale),
        grid=(b, n_heads, s // t),
        in_specs=[
            pl.BlockSpec((1, t, d), lambda bi, h, i: (bi, i, h)),
            pl.BlockSpec((1, s, d), lambda bi, h, i: (bi, 0, h)),
            pl.BlockSpec((1, s, d), lambda bi, h, i: (bi, 0, h)),
        ],
        out_specs=pl.BlockSpec((1, t, d), lambda bi, h, i: (bi, i, h)),
        out_shape=jax.ShapeDtypeStruct((b, s, n_heads * d), F32),
        scratch_shapes=[pltpu.VMEM((t, 1), F32), pltpu.VMEM((t, d), F32)],
        compiler_params=_cparams(3),
        name="sb_attn",
    )(q, k, v)


def _sb_attn_cached_kernel(q_ref, kc_ref, vc_ref, kn_ref, vn_ref, o_ref, c_ref, acc_ref, *, tk, scale):
    q = q_ref[0]
    sn = q.shape[0]
    n_blocks = kc_ref.shape[1] // tk
    c_ref[...] = jnp.zeros_like(c_ref)
    acc_ref[...] = jnp.zeros_like(acc_ref)
    _sb_step(q, kn_ref[0], vn_ref[0], _suffix_matrix(sn), c_ref, acc_ref, scale, True)
    u = _suffix_matrix(tk)

    def body(jj, carry):
        ks = pl.multiple_of((n_blocks - 1 - jj) * tk, tk)
        k = kc_ref[0, pl.ds(ks, tk), :].astype(BF16)
        v = vc_ref[0, pl.ds(ks, tk), :].astype(BF16)
        _sb_step(q, k, v, u, c_ref, acc_ref, scale, False)
        return carry

    lax.fori_loop(0, n_blocks, body, 0)
    o_ref[0] = acc_ref[...]


def _sb_attn_cached(q, kc, vc, kn, vn, n_heads, scale):
    b, sn, _ = q.shape
    past = kc.shape[1]
    d = HEAD_DIM
    tk = _tile(past, 256, LANES)
    new = pl.BlockSpec((1, sn, d), lambda bi, h: (bi, 0, h))
    old = pl.BlockSpec((1, past, d), lambda bi, h: (bi, 0, h))
    return pl.pallas_call(
        functools.partial(_sb_attn_cached_kernel, tk=tk, scale=scale),
        grid=(b, n_heads),
        in_specs=[new, old, old, new, new],
        out_specs=new,
        out_shape=jax.ShapeDtypeStruct((b, sn, n_heads * d), F32),
        scratch_shapes=[pltpu.VMEM((sn, 1), F32), pltpu.VMEM((sn, d), F32)],
        compiler_params=_cparams(2),
        name="sb_attn_cached",
    )(q, kc, vc, kn, vn)


def _group_norm_kernel(of_ref, os_ref, om_ref, g_ref, h_ref):
    off = 0
    for ref in (of_ref, os_ref, om_ref):
        w = ref.shape[-1]
        h_ref[0, :, off:off + w] = (_rms(ref[0]) * g_ref[:, off:off + w]).astype(BF16)
        off += w


def _group_norm(o_fox, o_sb, o_mla, g):
    b, s, _ = o_fox.shape
    d = g.shape[-1]
    tr = _tile(s, 256, SUBLANES)
    rows = lambda a: pl.BlockSpec((1, tr, a.shape[-1]), lambda bi, i: (bi, i, 0))
    return pl.pallas_call(
        _group_norm_kernel,
        grid=(b, s // tr),
        in_specs=[rows(o_fox), rows(o_sb), rows(o_mla), pl.BlockSpec((1, d), lambda bi, i: (0, 0))],
        out_specs=pl.BlockSpec((1, tr, d), lambda bi, i: (bi, i, 0)),
        out_shape=jax.ShapeDtypeStruct((b, s, d), BF16),
        compiler_params=_cparams(2),
        name="group_norm",
    )(o_fox, o_sb, o_mla, g.reshape(1, d))


def _rope_tables(pos, rope_dim):
    half = rope_dim // 2
    inv = ROPE_THETA ** (-jnp.arange(half, dtype=F32) / half)
    ang = pos.astype(F32)[:, None] * inv[None, :]
    cos, sin = jnp.cos(ang), jnp.sin(ang)
    n = pos.shape[0]
    pad = LANES - rope_dim
    cos_t = jnp.concatenate([cos, cos, jnp.ones((n, pad), F32)], axis=-1)
    sin_t = jnp.concatenate([-sin, sin, jnp.zeros((n, pad), F32)], axis=-1)
    return cos_t, sin_t


def _prep_layer_weights(l, dims, w_in, b_forget, w_q_up, w_kv_up, w_out, w_ffn_gate, w_ffn_up, w_ffn_down):
    hf, hs, hm, q_lora, kv_lora, rope = dims
    wf, ws = hf * HEAD_DIM, hs * HEAD_DIM
    d = w_in.shape[1]
    w = w_in[l]
    widths = (wf, wf, wf, hf, ws, ws, ws, q_lora, kv_lora, rope)
    offs = [0]
    for x in widths:
        offs.append(offs[-1] + x)
    col = lambda i: w[:, offs[i]:offs[i + 1]]
    lw = {}
    for name, i in (("fq", 0), ("fk", 1), ("fv", 2), ("sq", 4), ("sk", 5), ("sv", 6)):
        lw[name] = col(i).astype(BF16)
    pad = jnp.zeros((d, LANES - rope - hf), F32)
    lw["mla_in"] = jnp.concatenate([col(7), col(8), col(9), col(3), pad], axis=1).astype(BF16)
    lw["bf"] = jnp.zeros((1, LANES), F32).at[0, rope:rope + hf].set(b_forget[l])
    wq = w_q_up[l].reshape(q_lora, hm, MLA_NOPE + rope)
    wq = jnp.pad(wq, ((0, 0), (0, 0), (0, MLA_QK_PAD - MLA_NOPE - rope)))
    lw["wq"] = wq.reshape(q_lora, hm * MLA_QK_PAD).astype(BF16)
    wkv = w_kv_up[l].reshape(kv_lora, hm, MLA_NOPE + MLA_V)
    lw["wk"] = wkv[:, :, :MLA_NOPE].reshape(kv_lora, hm * MLA_NOPE).astype(BF16)
    lw["wv"] = wkv[:, :, MLA_NOPE:].reshape(kv_lora, hm * MLA_V).astype(BF16)
    lw["w_out"] = w_out[l].astype(BF16)
    for f in range(2):
        lw[f"wg{f}"] = w_ffn_gate[l, f].astype(BF16)
        lw[f"wu{f}"] = w_ffn_up[l, f].astype(BF16)
        lw[f"wd{f}"] = w_ffn_down[l, f].astype(BF16)
    return lw


def _ffn(h, lw, f):
    b, s, d = h.shape
    a = _swiglu(h.reshape(b * s, d), lw[f"wg{f}"], lw[f"wu{f}"])
    (y,) = _mm(a, lw[f"wd{f}"], (F32,), "ffn_down")
    return y.reshape(b, s, d)


def _token_mix(h, lw, dims, tables, cache):
    hf, hs, hm, q_lora, kv_lora, rope = dims
    b, s, d = h.shape
    h2 = h.reshape(b * s, d)
    cos_t, sin_t = tables
    three = lambda a: a.reshape(b, s, -1)

    (fq,) = _mm(h2, lw["fq"], (BF16,), "in_fox_q")
    fk32, fk16 = _mm(h2, lw["fk"], (F32, BF16), "in_fox_k")
    fv32, fv16 = _mm(h2, lw["fv"], (F32, BF16), "in_fox_v")
    (sq,) = _mm(h2, lw["sq"], (BF16,), "in_sb_q")
    sk32, sk16 = _mm(h2, lw["sk"], (F32, BF16), "in_sb_k")
    sv32, sv16 = _mm(h2, lw["sv"], (F32, BF16), "in_sb_v")
    (mlat,) = _mm(h2, lw["mla_in"], (F32,), "in_mla")

    qlat, ckv32, ckv16, krlf = _mla_prep(mlat, lw["gq"], lw["gkv"], lw["bf"], cos_t, sin_t, q_lora, kv_lora, hf)
    q_mla = _q_up(qlat, lw["wq"], cos_t, sin_t, hm)
    k_mla, v_mla = _kv_up(ckv16, lw["wk"], lw["wv"], krlf, hm)

    krlf3 = three(krlf)
    krope = krlf3[:, :, :rope]
    logf = krlf3[:, :, rope:rope + hf]
    fox_scale = HEAD_DIM ** -0.5
    mla_scale = (MLA_NOPE + rope) ** -0.5

    if cache is None:
        f_all = _cumsum_rows(krlf3)[:, :, rope:rope + hf]
        f_bh = jnp.swapaxes(f_all, 1, 2)
        o_fox = _softmax_attn(three(fq), three(fk16), three(fv16), hf, HEAD_DIM, HEAD_DIM, fox_scale, 0,
                              f_col=f_bh[..., None], f_row=f_bh)
        o_sb = _sb_attn(three(sq), three(sk16), three(sv16), hs, fox_scale)
        o_mla = _softmax_attn(three(q_mla), three(k_mla), three(v_mla), hm, MLA_QK_PAD, MLA_V, mla_scale, CHUNK)
    else:
        c_fk, c_fv, c_logf, c_sk, c_sv, c_ckv, c_krope = cache
        past = c_fk.shape[1]
        flat = lambda a: a.reshape(b, past, -1)
        logf_all = jnp.concatenate([c_logf.astype(F32), logf], axis=1)
        logf_all = jnp.pad(logf_all, ((0, 0), (0, 0), (0, LANES - hf)))
        f_all = _cumsum_rows(logf_all)[:, :, :hf]
        f_bh = jnp.swapaxes(f_all, 1, 2)
        o_fox = _softmax_attn_cached(
            three(fq), flat(c_fk), flat(c_fv), three(fk16), three(fv16), hf, HEAD_DIM, HEAD_DIM, fox_scale, 0,
            f_col=f_bh[:, :, past:, None], f_row_c=f_bh[:, :, None, :past], f_row_n=f_bh[:, :, None, past:])
        o_sb = _sb_attn_cached(three(sq), flat(c_sk), flat(c_sv), three(sk16), three(sv16), hs, fox_scale)
        c_kr = jnp.pad(c_krope.reshape(b * past, rope), ((0, 0), (0, LANES - rope)))
        kc_mla, vc_mla = _kv_up(c_ckv.reshape(b * past, kv_lora), lw["wk"], lw["wv"], c_kr, hm)
        o_mla = _softmax_attn_cached(
            three(q_mla), flat(kc_mla), flat(vc_mla), three(k_mla), three(v_mla), hm, MLA_QK_PAD, MLA_V,
            mla_scale, CHUNK)

    hmix = _group_norm(o_fox, o_sb, o_mla, lw["g_mix"])
    (mix,) = _mm(hmix.reshape(b * s, d), lw["w_out"], (F32,), "mix_out")
    new_state = (
        fk32.reshape(b, s, hf, HEAD_DIM), fv32.reshape(b, s, hf, HEAD_DIM), logf,
        sk32.reshape(b, s, hs, HEAD_DIM), sv32.reshape(b, s, hs, HEAD_DIM),
        ckv32.reshape(b, s, kv_lora), krope,
    )
    return mix.reshape(b, s, d), new_state


def kernel(x_prompt, x_sample, c_prompt, c_sample, cache_fox_k, cache_fox_v, cache_fox_logf, cache_sb_k, cache_sb_v, cache_mla_ckv, cache_mla_krope, w_ada, b_ada, g_norm, w_in, b_forget, g_q_lat, w_q_up, g_kv_lat, w_kv_up, g_mix_out, w_out, w_ffn_gate, w_ffn_up, w_ffn_down, g_final):
    depth = w_ada.shape[0]
    d = x_prompt.shape[-1]
    nbp, nbs = x_prompt.shape[0], x_sample.shape[0]
    past = cache_fox_k.shape[2]
    hf, hs = cache_fox_k.shape[3], cache_sb_k.shape[3]
    kv_lora, rope, q_lora = cache_mla_ckv.shape[-1], cache_mla_krope.shape[-1], g_q_lat.shape[-1]
    hm = w_q_up.shape[-1] // (MLA_NOPE + rope)
    dims = (hf, hs, hm, q_lora, kv_lora, rope)

    c_all = jnp.concatenate([c_prompt, c_sample], axis=0)
    nb = nbp + nbs
    nb_pad = -(-nb // SUBLANES) * SUBLANES
    c_all = jnp.pad(c_all, ((0, nb_pad - nb), (0, 0)))
    mods = _ada(c_all, w_ada, b_ada).reshape(depth, nb_pad, N_MOD, 1, d)
    groups = {"p": (0, nbp), "s": (nbp, nb)}
    mod = lambda l, grp, i: mods[l, groups[grp][0]:groups[grp][1], i]

    tables = {
        "p": _rope_tables(jnp.arange(x_prompt.shape[1]), rope),
        "s": _rope_tables(past + jnp.arange(x_sample.shape[1]), rope),
    }
    caches = (cache_fox_k, cache_fox_v, cache_fox_logf, cache_sb_k, cache_sb_v, cache_mla_ckv, cache_mla_krope)

    xs = {"p": x_prompt, "s": x_sample}
    pending = {"p": None, "s": None}
    states = {"p": [], "s": []}
    for l in range(depth):
        lw = _prep_layer_weights(l, dims, w_in, b_forget, w_q_up, w_kv_up, w_out, w_ffn_gate, w_ffn_up, w_ffn_down)
        lw["gq"], lw["gkv"], lw["g_mix"] = g_q_lat[l], g_kv_lat[l], g_mix_out[l]
        for grp in ("p", "s"):
            m = lambda i: mod(l, grp, i)
            x, h = _prenorm(xs[grp], g_norm[l, 0], m(1), m(0), res=pending[grp])
            y = _ffn(h, lw, 0)
            x, h = _prenorm(x, g_norm[l, 1], m(4), m(3), res=(y, m(2), 0.5))
            cache = None if grp == "p" else tuple(cc[l] for cc in caches)
            mix, st = _token_mix(h, lw, dims, tables[grp], cache)
            states[grp].append(st)
            x, h = _prenorm(x, g_norm[l, 2], m(7), m(6), res=(mix, m(5), 1.0))
            y = _ffn(h, lw, 1)
            xs[grp] = x
            pending[grp] = (y, m(8), 0.5)

    y_prompt = _prenorm(xs["p"], g_final, res=pending["p"], final=True)
    y_sample = _prenorm(xs["s"], g_final, res=pending["s"], final=True)
    stack = lambda grp, i: jnp.stack([st[i] for st in states[grp]])
    return (y_prompt, y_sample) + tuple(stack("p", i) for i in range(7)) + tuple(stack("s", i) for i in range(7))
```

```python
import functools

import jax
import jax.numpy as jnp
from jax import lax
from jax.experimental import pallas as pl
from jax.experimental.pallas import tpu as pltpu

F32 = jnp.float32
BF16 = jnp.bfloat16

HEAD_DIM = 128
MLA_NOPE = 128
MLA_V = 128
CHUNK = 64
N_MOD = 9
EPS = 1e-6
ROPE_THETA = 10000.0

LANES = 128
SUBLANES = 8
MLA_QK_PAD = 256
VMEM_LIMIT = 56 * 1024 * 1024
LOG2E = 1.4426950408889634

ATTN_TILE = 512
ATTN_HEAD_GROUP = 2
SB_SUB_TILE = 256


def _cparams(n_axes, sequential_axes=()):
    sem = tuple("arbitrary" if a in sequential_axes else "parallel" for a in range(n_axes))
    return pltpu.CompilerParams(dimension_semantics=sem, vmem_limit_bytes=VMEM_LIMIT)


def _tile(n, pref, align):
    if n <= pref:
        return n
    for t in range(pref - pref % align, 0, -align):
        if n % t == 0:
            return t
    return n


def _dot(a, b):
    return jnp.dot(a, b, preferred_element_type=F32)


def _dot_nt(a, b):
    return lax.dot_general(a, b, (((1,), (1,)), ((), ())), preferred_element_type=F32)


def _softplus_neg_abs(z):
    return jnp.log1p(jnp.exp(-jnp.abs(z)))


def _rms(x):
    return x * lax.rsqrt(jnp.mean(x * x, axis=-1, keepdims=True) + EPS)


def _rope_tile(t, cos, sin):
    lane = lax.broadcasted_iota(jnp.int32, t.shape, 1)
    half = LANES // 4
    partner = jnp.where(lane < half, pltpu.roll(t, LANES - half, 1), pltpu.roll(t, half, 1))
    return t * cos + partner * sin


def _ada_kernel(c_ref, w_ref, b_ref, o_ref):
    c = c_ref[...]
    a = (c / (1.0 + jnp.exp(-c))).astype(BF16)
    o_ref[0] = _dot(a, w_ref[0].astype(BF16)) + b_ref[0]


def _ada(c_all, w_ada, b_ada):
    depth, d, n = w_ada.shape
    nb = c_all.shape[0]
    tn = _tile(n, 512, LANES)
    return pl.pallas_call(
        _ada_kernel,
        grid=(depth, n // tn),
        in_specs=[
            pl.BlockSpec((nb, d), lambda l, j: (0, 0)),
            pl.BlockSpec((1, d, tn), lambda l, j: (l, 0, j)),
            pl.BlockSpec((1, 1, tn), lambda l, j: (l, 0, j)),
        ],
        out_specs=pl.BlockSpec((1, nb, tn), lambda l, j: (l, 0, j)),
        out_shape=jax.ShapeDtypeStruct((depth, nb, n), F32),
        compiler_params=_cparams(2),
        name="ada",
    )(c_all, w_ada, b_ada.reshape(depth, 1, n))


def _prenorm_kernel(*refs, coef, has_res, final):
    refs = list(refs)
    x_ref = refs.pop(0)
    x = x_ref[0]
    if has_res:
        y_ref = refs.pop(0)
        gate_ref = refs.pop(0)
        x = x + (coef * gate_ref[0]) * y_ref[0]
    g_ref = refs.pop(0)
    if final:
        (o_ref,) = refs
        o_ref[0] = _rms(x) * g_ref[...]
        return
    sc_ref = refs.pop(0)
    sh_ref = refs.pop(0)
    if has_res:
        xo_ref = refs.pop(0)
        xo_ref[0] = x
    (h_ref,) = refs
    h = (_rms(x) * g_ref[...]) * (1.0 + sc_ref[0]) + sh_ref[0]
    h_ref[0] = h.astype(BF16)


def _prenorm(x, g, scale=None, shift=None, res=None, final=False):
    b, s, d = x.shape
    tr = _tile(s, 256, SUBLANES)
    row = pl.BlockSpec((1, tr, d), lambda bi, i: (bi, i, 0))
    per_b = pl.BlockSpec((1, 1, d), lambda bi, i: (bi, 0, 0))
    shared = pl.BlockSpec((1, d), lambda bi, i: (0, 0))
    args, specs = [x], [row]
    coef = 0.0
    if res is not None:
        y, gate, coef = res
        args += [y, gate]
        specs += [row, per_b]
    args.append(g.reshape(1, d))
    specs.append(shared)
    out_shape, out_specs = [], []
    if final:
        out_shape.append(jax.ShapeDtypeStruct((b, s, d), F32))
        out_specs.append(row)
    else:
        args += [scale, shift]
        specs += [per_b, per_b]
        if res is not None:
            out_shape.append(jax.ShapeDtypeStruct((b, s, d), F32))
            out_specs.append(row)
        out_shape.append(jax.ShapeDtypeStruct((b, s, d), BF16))
        out_specs.append(row)
    outs = pl.pallas_call(
        functools.partial(_prenorm_kernel, coef=coef, has_res=res is not None, final=final),
        grid=(b, s // tr),
        in_specs=specs,
        out_specs=out_specs,
        out_shape=out_shape,
        compiler_params=_cparams(2),
        name="final_norm" if final else "prenorm",
    )(*args)
    if final:
        return outs[0]
    if res is None:
        return x, outs[0]
    return outs[0], outs[1]


def _mm_kernel(a_ref, w_ref, *o_refs, out_scale):
    acc = _dot(a_ref[...].astype(BF16), w_ref[...])
    if out_scale != 1.0:
        acc = acc * out_scale
    for o_ref in o_refs:
        o_ref[...] = acc.astype(o_ref.dtype)


def _mm(a, w, out_dtypes, name, tm_pref=1024, tn_pref=512, out_scale=1.0):
    m, k = a.shape
    n = w.shape[1]
    tm = _tile(m, tm_pref, SUBLANES)
    tn = _tile(n, tn_pref, LANES)
    outs = pl.pallas_call(
        functools.partial(_mm_kernel, out_scale=out_scale),
        grid=(m // tm, n // tn),
        in_specs=[
            pl.BlockSpec((tm, k), lambda i, j: (i, 0)),
            pl.BlockSpec((k, tn), lambda i, j: (0, j)),
        ],
        out_specs=[pl.BlockSpec((tm, tn), lambda i, j: (i, j)) for _ in out_dtypes],
        out_shape=[jax.ShapeDtypeStruct((m, n), dt) for dt in out_dtypes],
        compiler_params=_cparams(2),
        name=name,
    )(a, w)
    return outs


def _swiglu_kernel(a_ref, wg_ref, wu_ref, o_ref, *, rows):
    for r0 in range(0, a_ref.shape[0], rows):
        a = a_ref[r0:r0 + rows, :]
        g = _dot(a, wg_ref[...])
        u = _dot(a, wu_ref[...])
        o_ref[r0:r0 + rows, :] = ((g / (1.0 + jnp.exp(-g))) * u).astype(o_ref.dtype)


def _swiglu(a, wg, wu, tm_pref=2048, tn_pref=256, rows_pref=512):
    m, k = a.shape
    n = wg.shape[1]
    tm = _tile(m, tm_pref, SUBLANES)
    tn = _tile(n, tn_pref, LANES)
    rows = _tile(tm, rows_pref, SUBLANES)
    wspec = pl.BlockSpec((k, tn), lambda i, j: (0, j))
    return pl.pallas_call(
        functools.partial(_swiglu_kernel, rows=rows),
        grid=(m // tm, n // tn),
        in_specs=[pl.BlockSpec((tm, k), lambda i, j: (i, 0)), wspec, wspec],
        out_specs=pl.BlockSpec((tm, tn), lambda i, j: (i, j)),
        out_shape=jax.ShapeDtypeStruct((m, n), BF16),
        compiler_params=_cparams(2),
        name="ffn_swiglu",
    )(a, wg, wu)


def _mla_prep_kernel(m_ref, gq_ref, gkv_ref, bf_ref, cos_ref, sin_ref,
                     ql_ref, ckv32_ref, ckv16_ref, kr_ref, *, q_lora, kv_lora, n_forget):
    x = m_ref[...]
    ql_ref[...] = (_rms(x[:, :q_lora]) * gq_ref[...]).astype(BF16)
    ckv = _rms(x[:, q_lora:q_lora + kv_lora]) * gkv_ref[...]
    ckv32_ref[...] = ckv
    ckv16_ref[...] = ckv.astype(BF16)
    t = x[:, q_lora + kv_lora:]
    kr = _rope_tile(t, cos_ref[...], sin_ref[...])
    z = t + bf_ref[...]
    logf = jnp.minimum(z, 0.0) - _softplus_neg_abs(z)
    lane = lax.broadcasted_iota(jnp.int32, t.shape, 1)
    rope_w = LANES // 2
    kr_ref[...] = jnp.where(lane < rope_w, kr, jnp.where(lane < rope_w + n_forget, logf, 0.0))


def _mla_prep(mlat, gq, gkv, bf_tile, cos_t, sin_t, q_lora, kv_lora, n_forget):
    m = mlat.shape[0]
    s = cos_t.shape[0]
    tr = _tile(s, 256, SUBLANES)
    ns = s // tr
    rows = lambda w: pl.BlockSpec((tr, w), lambda i: (i, 0))
    shared = lambda w: pl.BlockSpec((1, w), lambda i: (0, 0))
    table = pl.BlockSpec((tr, LANES), lambda i: (i % ns, 0))
    return pl.pallas_call(
        functools.partial(_mla_prep_kernel, q_lora=q_lora, kv_lora=kv_lora, n_forget=n_forget),
        grid=(m // tr,),
        in_specs=[rows(mlat.shape[1]), shared(q_lora), shared(kv_lora), shared(LANES), table, table],
        out_specs=[rows(q_lora), rows(kv_lora), rows(kv_lora), rows(LANES)],
        out_shape=[
            jax.ShapeDtypeStruct((m, q_lora), BF16),
            jax.ShapeDtypeStruct((m, kv_lora), F32),
            jax.ShapeDtypeStruct((m, kv_lora), BF16),
            jax.ShapeDtypeStruct((m, LANES), F32),
        ],
        compiler_params=_cparams(1),
        name="mla_prep",
    )(mlat, gq.reshape(1, q_lora), gkv.reshape(1, kv_lora), bf_tile, cos_t, sin_t)


def _q_up_kernel(a_ref, w_ref, cos_ref, sin_ref, o_ref, *, n_heads, out_scale):
    acc = _dot(a_ref[...], w_ref[...]) * out_scale
    cos = cos_ref[...]
    sin = sin_ref[...]
    for h in range(n_heads):
        base = h * MLA_QK_PAD
        o_ref[:, base:base + MLA_NOPE] = acc[:, base:base + MLA_NOPE].astype(BF16)
        t = acc[:, base + MLA_NOPE:base + MLA_QK_PAD]
        o_ref[:, base + MLA_NOPE:base + MLA_QK_PAD] = _rope_tile(t, cos, sin).astype(BF16)


def _q_up(qlat, wq, cos_t, sin_t, n_heads, out_scale):
    m, k = qlat.shape
    n = wq.shape[1]
    s = cos_t.shape[0]
    tr = _tile(s, 512, SUBLANES)
    ns = s // tr
    table = pl.BlockSpec((tr, LANES), lambda i: (i % ns, 0))
    return pl.pallas_call(
        functools.partial(_q_up_kernel, n_heads=n_heads, out_scale=out_scale),
        grid=(m // tr,),
        in_specs=[pl.BlockSpec((tr, k), lambda i: (i, 0)), pl.BlockSpec((k, n), lambda i: (0, 0)), table, table],
        out_specs=pl.BlockSpec((tr, n), lambda i: (i, 0)),
        out_shape=jax.ShapeDtypeStruct((m, n), BF16),
        compiler_params=_cparams(1),
        name="mla_q_up",
    )(qlat, wq, cos_t, sin_t)


def _kv_up_kernel(a_ref, wk_ref, wv_ref, kr_ref, k_ref, v_ref, *, n_heads):
    a = a_ref[...].astype(BF16)
    kn = _dot(a, wk_ref[...])
    v_ref[...] = _dot(a, wv_ref[...]).astype(BF16)
    kr = kr_ref[...]
    lane = lax.broadcasted_iota(jnp.int32, kr.shape, 1)
    krm = jnp.where(lane < LANES // 2, kr, 0.0).astype(BF16)
    for h in range(n_heads):
        base = h * MLA_QK_PAD
        k_ref[:, base:base + MLA_NOPE] = kn[:, h * MLA_NOPE:(h + 1) * MLA_NOPE].astype(BF16)
        k_ref[:, base + MLA_NOPE:base + MLA_QK_PAD] = krm


def _kv_up(ckv, wk, wv, kr, n_heads):
    m, k = ckv.shape
    tm = _tile(m, 512, SUBLANES)
    rows = lambda w: pl.BlockSpec((tm, w), lambda i: (i, 0))
    whole = lambda a: pl.BlockSpec(a.shape, lambda i: (0, 0))
    return pl.pallas_call(
        functools.partial(_kv_up_kernel, n_heads=n_heads),
        grid=(m // tm,),
        in_specs=[rows(k), whole(wk), whole(wv), rows(LANES)],
        out_specs=[rows(n_heads * MLA_QK_PAD), rows(n_heads * MLA_V)],
        out_shape=[
            jax.ShapeDtypeStruct((m, n_heads * MLA_QK_PAD), BF16),
            jax.ShapeDtypeStruct((m, n_heads * MLA_V), BF16),
        ],
        compiler_params=_cparams(1),
        name="mla_kv_up",
    )(ckv, wk, wv, kr)


def _split3(x):
    x1 = x.astype(BF16)
    r1 = x - x1.astype(F32)
    x2 = r1.astype(BF16)
    x3 = (r1 - x2.astype(F32)).astype(BF16)
    return x1, x2, x3


def _cumsum_kernel(x_ref, o_ref, *rest, tc, with_parts):
    carry_ref = rest[-1]

    @pl.when(pl.program_id(1) == 0)
    def _():
        carry_ref[...] = jnp.zeros_like(carry_ref)

    r = lax.broadcasted_iota(jnp.int32, (tc, tc), 0)
    c = lax.broadcasted_iota(jnp.int32, (tc, tc), 1)
    tri = jnp.where(c <= r, 1.0, 0.0).astype(BF16)
    x1, x2, x3 = _split3(x_ref[0])
    y = (_dot(tri, x3) + _dot(tri, x2)) + _dot(tri, x1) + carry_ref[...]
    o_ref[0] = y
    carry_ref[...] = y[tc - 1:tc, :]
    if with_parts:
        for part_ref, part in zip(rest[:3], _split3(y * (-LOG2E))):
            part_ref[0] = part


def _cumsum_rows(x, with_parts=False):
    b, s, w = x.shape
    tc = _tile(s, 256, 2 * SUBLANES)
    blk = pl.BlockSpec((1, tc, w), lambda bi, j: (bi, j, 0))
    n_parts = 3 if with_parts else 0
    outs = pl.pallas_call(
        functools.partial(_cumsum_kernel, tc=tc, with_parts=with_parts),
        grid=(b, s // tc),
        in_specs=[blk],
        out_specs=[blk] * (1 + n_parts),
        out_shape=[jax.ShapeDtypeStruct((b, s, w), F32)] + [jax.ShapeDtypeStruct((b, s, w), BF16)] * n_parts,
        scratch_shapes=[pltpu.VMEM((1, w), F32)],
        compiler_params=_cparams(2, sequential_axes=(1,)),
        name="cumsum_rows",
    )(x)
    return outs if with_parts else outs[0]


def _rep_lanes(x, width):
    return x if width == LANES else jnp.concatenate([x] * (width // LANES), axis=1)


def _softmax_attn_kernel(qt_ref, k_ref, vt_ref, o_ref, m_ref, l_ref, acc_ref, *, t, group, dq, dv, chunk):
    i = pl.program_id(2)
    m_ref[...] = jnp.full_like(m_ref, -jnp.inf)
    l_ref[...] = jnp.zeros_like(l_ref)
    acc_ref[...] = jnp.zeros_like(acc_ref)

    def step(j, masked):
        ks = pl.multiple_of(j * t, t)
        heads = range(group)
        scores = [_dot(k_ref[0, pl.ds(ks, t), g * dq:(g + 1) * dq], qt_ref[0, g * dq:(g + 1) * dq, :])
                  for g in heads]
        probs, alphas = [], []
        for g in heads:
            s = scores[g]
            if masked:
                kpos = lax.broadcasted_iota(jnp.int32, s.shape, 0)
                qpos = lax.broadcasted_iota(jnp.int32, s.shape, 1)
                visible = (kpos // chunk) <= (qpos // chunk) if chunk else kpos <= qpos
                s = jnp.where(visible, s, -jnp.inf)
            m_prev = m_ref[g]
            m_new = jnp.maximum(m_prev, jnp.max(s, axis=0, keepdims=True))
            alpha = jnp.exp2(m_prev - m_new)
            p = jnp.exp2(s - m_new)
            l_ref[g] = alpha * l_ref[g] + jnp.sum(p, axis=0, keepdims=True)
            m_ref[g] = m_new
            probs.append(p.astype(BF16))
            alphas.append(alpha)
        for g in heads:
            pv = _dot(vt_ref[0, g * dv:(g + 1) * dv, pl.ds(ks, t)], probs[g])
            acc_ref[g] = alphas[g] * acc_ref[g] + pv

    def body(j, carry):
        step(j, False)
        return carry

    lax.fori_loop(0, i, body, 0)
    step(i, True)
    for g in range(group):
        o_ref[0, :, g * dv:(g + 1) * dv] = (acc_ref[g] / l_ref[g]).T


def _head_group(n_heads):
    return ATTN_HEAD_GROUP if n_heads % ATTN_HEAD_GROUP == 0 else 1


def _softmax_attn(qt, k, vt, n_heads, dq, dv, chunk, name):
    b, s, _ = k.shape
    t = _tile(s, ATTN_TILE, LANES)
    grp = _head_group(n_heads)
    stat = pltpu.VMEM((grp, 1, t), F32)
    return pl.pallas_call(
        functools.partial(_softmax_attn_kernel, t=t, group=grp, dq=dq, dv=dv, chunk=chunk),
        grid=(b, n_heads // grp, s // t),
        in_specs=[
            pl.BlockSpec((1, grp * dq, t), lambda bi, h, i: (bi, h, i)),
            pl.BlockSpec((1, s, grp * dq), lambda bi, h, i: (bi, 0, h)),
            pl.BlockSpec((1, grp * dv, s), lambda bi, h, i: (bi, h, 0)),
        ],
        out_specs=pl.BlockSpec((1, t, grp * dv), lambda bi, h, i: (bi, i, h)),
        out_shape=jax.ShapeDtypeStruct((b, s, n_heads * dv), F32),
        scratch_shapes=[stat, stat, pltpu.VMEM((grp, dv, t), F32)],
        compiler_params=_cparams(3),
        name=name,
    )(qt, k, vt)


def _softmax_attn_cached_kernel(*refs, has_bias, chunk, past):
    if has_bias:
        q_ref, kc_ref, vc_ref, kn_ref, vn_ref, fkc_ref, fkn_ref, o_ref = refs
    else:
        q_ref, kc_ref, vc_ref, kn_ref, vn_ref, o_ref = refs
    q = q_ref[0]
    s_c = _dot_nt(q, kc_ref[0].astype(BF16))
    s_n = _dot_nt(q, kn_ref[0])
    if has_bias:
        s_c = s_c - fkc_ref[0, 0] * LOG2E
        s_n = s_n - fkn_ref[0, 0] * LOG2E
    r = lax.broadcasted_iota(jnp.int32, s_n.shape, 0) + past
    c = lax.broadcasted_iota(jnp.int32, s_n.shape, 1) + past
    visible = (c // chunk) <= (r // chunk) if chunk else c <= r
    s_n = jnp.where(visible, s_n, -jnp.inf)
    m = jnp.maximum(jnp.max(s_c, axis=-1, keepdims=True), jnp.max(s_n, axis=-1, keepdims=True))
    p_c = jnp.exp2(s_c - m)
    p_n = jnp.exp2(s_n - m)
    l = jnp.sum(p_c, axis=-1, keepdims=True) + jnp.sum(p_n, axis=-1, keepdims=True)
    acc = _dot(p_c.astype(BF16), vc_ref[0].astype(BF16)) + _dot(p_n.astype(BF16), vn_ref[0])
    o_ref[0] = acc / l


def _softmax_attn_cached(q, kc, vc, kn, vn, n_heads, dq, dv, chunk, f_row_c=None, f_row_n=None):
    b, sn, _ = q.shape
    past = kc.shape[1]
    has_bias = f_row_c is not None
    new = lambda d: pl.BlockSpec((1, sn, d), lambda bi, h: (bi, 0, h))
    old = lambda d: pl.BlockSpec((1, past, d), lambda bi, h: (bi, 0, h))
    in_specs = [new(dq), old(dq), old(dv), new(dq), new(dv)]
    args = [q, kc, vc, kn, vn]
    if has_bias:
        in_specs += [
            pl.BlockSpec((1, 1, 1, past), lambda bi, h: (bi, h, 0, 0)),
            pl.BlockSpec((1, 1, 1, sn), lambda bi, h: (bi, h, 0, 0)),
        ]
        args += [f_row_c, f_row_n]
    return pl.pallas_call(
        functools.partial(_softmax_attn_cached_kernel, has_bias=has_bias, chunk=chunk, past=past),
        grid=(b, n_heads),
        in_specs=in_specs,
        out_specs=new(dv),
        out_shape=jax.ShapeDtypeStruct((b, sn, n_heads * dv), F32),
        compiler_params=_cparams(2),
        name="fox_attn_cached" if has_bias else "mla_attn_cached",
    )(*args)


def _suffix_matrix(t):
    r = lax.broadcasted_iota(jnp.int32, (t, t), 0)
    c = lax.broadcasted_iota(jnp.int32, (t, t), 1)
    return jnp.where(r >= c, 1.0, 0.0).astype(BF16)


def _sb_fold(q, k, v, u, c, strict):
    z = _dot_nt(q, k)
    neg_abs = lax.bitcast_convert_type(lax.bitcast_convert_type(z, jnp.uint32) | jnp.uint32(0x80000000), F32)
    sp = jnp.maximum(z, 0.0) + jnp.log(1.0 + jnp.exp2(neg_abs)) * LOG2E
    if strict is not None:
        sp = jnp.where(strict, sp, 0.0)
    hi = lax.bitcast_convert_type(lax.bitcast_convert_type(sp, jnp.uint32) & jnp.uint32(0xFFFF0000), F32)
    lo = (sp - hi).astype(BF16)
    suffix = _dot(lo, u) + _dot(hi.astype(BF16), u)
    width = z.shape[1]
    c_wide = c[:, :width] if width < LANES else _rep_lanes(c, width)
    a = jnp.exp2(z - (suffix + c_wide))
    if strict is not None:
        a = jnp.where(strict, a, 0.0)
    return _dot(a.astype(BF16), v), c + jnp.sum(sp, axis=-1, keepdims=True)


def _sb_attn_kernel(qt_ref, k_ref, vt_ref, o_ref, c_ref, acc_ref, *, t, tsub, group, d):
    i = pl.program_id(2)
    r = lax.broadcasted_iota(jnp.int32, (tsub, tsub), 0)
    col = lax.broadcasted_iota(jnp.int32, (tsub, tsub), 1)
    ut = jnp.where(col >= r, 1.0, 0.0).astype(BF16)
    c_ref[...] = jnp.zeros_like(c_ref)
    acc_ref[...] = jnp.zeros_like(acc_ref)

    def step(j, masked):
        chains = [(g, sub) for g in range(group) for sub in reversed(range(t // tsub))]
        start = {sub: pl.multiple_of(j * t + sub * tsub, tsub) for _, sub in chains}
        z, sp, strict, suffix = {}, {}, {}, {}
        for ch in chains:
            g, sub = ch
            z[ch] = _dot(k_ref[0, pl.ds(start[sub], tsub), g * d:(g + 1) * d], qt_ref[0, g * d:(g + 1) * d, :])
        for ch in chains:
            g, sub = ch
            neg_abs = lax.bitcast_convert_type(
                lax.bitcast_convert_type(z[ch], jnp.uint32) | jnp.uint32(0x80000000), F32)
            sp[ch] = jnp.maximum(z[ch], 0.0) + jnp.log(1.0 + jnp.exp2(neg_abs)) * LOG2E
            if masked:
                kpos = lax.broadcasted_iota(jnp.int32, (tsub, t), 0) + sub * tsub
                qpos = lax.broadcasted_iota(jnp.int32, (tsub, t), 1)
                strict[ch] = kpos < qpos
                sp[ch] = jnp.where(strict[ch], sp[ch], 0.0)
            hi = lax.bitcast_convert_type(
                lax.bitcast_convert_type(sp[ch], jnp.uint32) & jnp.uint32(0xFFFF0000), F32)
            lo = (sp[ch] - hi).astype(BF16)
            suffix[ch] = _dot(ut, lo) + _dot(ut, hi.astype(BF16))
        c = {g: c_ref[g] for g in range(group)}
        acc = {g: acc_ref[g] for g in range(group)}
        for ch in chains:
            g, sub = ch
            a = jnp.exp2(z[ch] - (suffix[ch] + c[g]))
            if masked:
                a = jnp.where(strict[ch], a, 0.0)
            acc[g] = acc[g] + _dot(vt_ref[0, g * d:(g + 1) * d, pl.ds(start[sub], tsub)], a.astype(BF16))
            c[g] = c[g] + suffix[ch][0:1, :]
        for g in range(group):
            c_ref[g] = c[g]
            acc_ref[g] = acc[g]

    step(i, True)

    def body(jj, carry):
        step(i - 1 - jj, False)
        return carry

    lax.fori_loop(0, i, body, 0)
    for g in range(group):
        o_ref[0, :, g * d:(g + 1) * d] = acc_ref[g].T


def _sb_attn(qt, k, vt, n_heads):
    b, s, _ = k.shape
    d = HEAD_DIM
    t = _tile(s, ATTN_TILE, LANES)
    tsub = _tile(t, SB_SUB_TILE, LANES)
    grp = _head_group(n_heads)
    return pl.pallas_call(
        functools.partial(_sb_attn_kernel, t=t, tsub=tsub, group=grp, d=d),
        grid=(b, n_heads // grp, s // t),
        in_specs=[
            pl.BlockSpec((1, grp * d, t), lambda bi, h, i: (bi, h, i)),
            pl.BlockSpec((1, s, grp * d), lambda bi, h, i: (bi, 0, h)),
            pl.BlockSpec((1, grp * d, s), lambda bi, h, i: (bi, h, 0)),
        ],
        out_specs=pl.BlockSpec((1, t, grp * d), lambda bi, h, i: (bi, i, h)),
        out_shape=jax.ShapeDtypeStruct((b, s, n_heads * d), F32),
        scratch_shapes=[pltpu.VMEM((grp, 1, t), F32), pltpu.VMEM((grp, d, t), F32)],
        compiler_params=_cparams(3),
        name="sb_attn",
    )(qt, k, vt)


def _sb_attn_cached_kernel(q_ref, kc_ref, vc_ref, kn_ref, vn_ref, o_ref, *, tk):
    q = q_ref[0]
    sn = q.shape[0]
    r = lax.broadcasted_iota(jnp.int32, (sn, sn), 0)
    col = lax.broadcasted_iota(jnp.int32, (sn, sn), 1)
    acc, c = _sb_fold(q, kn_ref[0], vn_ref[0], _suffix_matrix(sn), jnp.zeros((sn, LANES), F32), col < r)
    u = _suffix_matrix(tk)
    for blk in reversed(range(kc_ref.shape[1] // tk)):
        k = kc_ref[0, blk * tk:(blk + 1) * tk, :].astype(BF16)
        v = vc_ref[0, blk * tk:(blk + 1) * tk, :].astype(BF16)
        pv, c = _sb_fold(q, k, v, u, c, None)
        acc = acc + pv
    o_ref[0] = acc


def _sb_attn_cached(q, kc, vc, kn, vn, n_heads):
    b, sn, _ = q.shape
    past = kc.shape[1]
    d = HEAD_DIM
    tk = _tile(past, SB_SUB_TILE, LANES)
    new = pl.BlockSpec((1, sn, d), lambda bi, h: (bi, 0, h))
    old = pl.BlockSpec((1, past, d), lambda bi, h: (bi, 0, h))
    return pl.pallas_call(
        functools.partial(_sb_attn_cached_kernel, tk=tk),
        grid=(b, n_heads),
        in_specs=[new, old, old, new, new],
        out_specs=new,
        out_shape=jax.ShapeDtypeStruct((b, sn, n_heads * d), F32),
        compiler_params=_cparams(2),
        name="sb_attn_cached",
    )(q, kc, vc, kn, vn)


def _group_norm_kernel(of_ref, os_ref, om_ref, g_ref, h_ref):
    off = 0
    for ref in (of_ref, os_ref, om_ref):
        w = ref.shape[-1]
        h_ref[0, :, off:off + w] = (_rms(ref[0]) * g_ref[:, off:off + w]).astype(BF16)
        off += w


def _group_norm(o_fox, o_sb, o_mla, g):
    b, s, _ = o_fox.shape
    d = g.shape[-1]
    tr = _tile(s, 256, SUBLANES)
    rows = lambda a: pl.BlockSpec((1, tr, a.shape[-1]), lambda bi, i: (bi, i, 0))
    return pl.pallas_call(
        _group_norm_kernel,
        grid=(b, s // tr),
        in_specs=[rows(o_fox), rows(o_sb), rows(o_mla), pl.BlockSpec((1, d), lambda bi, i: (0, 0))],
        out_specs=pl.BlockSpec((1, tr, d), lambda bi, i: (bi, i, 0)),
        out_shape=jax.ShapeDtypeStruct((b, s, d), BF16),
        compiler_params=_cparams(2),
        name="group_norm",
    )(o_fox, o_sb, o_mla, g.reshape(1, d))


def _rope_tables(pos, rope_dim):
    half = rope_dim // 2
    inv = ROPE_THETA ** (-jnp.arange(half, dtype=F32) / half)
    ang = pos.astype(F32)[:, None] * inv[None, :]
    cos, sin = jnp.cos(ang), jnp.sin(ang)
    n = pos.shape[0]
    pad = LANES - rope_dim
    cos_t = jnp.concatenate([cos, cos, jnp.ones((n, pad), F32)], axis=-1)
    sin_t = jnp.concatenate([-sin, sin, jnp.zeros((n, pad), F32)], axis=-1)
    return cos_t, sin_t


def _prep_layer_weights(l, dims, w_in, b_forget, w_q_up, w_kv_up, w_out, w_ffn_gate, w_ffn_up, w_ffn_down):
    hf, hs, hm, q_lora, kv_lora, rope = dims
    wf, ws = hf * HEAD_DIM, hs * HEAD_DIM
    d = w_in.shape[1]
    w = w_in[l]
    widths = (wf, wf, wf, hf, ws, ws, ws, q_lora, kv_lora, rope)
    offs = [0]
    for x in widths:
        offs.append(offs[-1] + x)
    col = lambda i: w[:, offs[i]:offs[i + 1]]
    lw = {}
    for name, i in (("fq", 0), ("fk", 1), ("fv", 2), ("sq", 4), ("sk", 5), ("sv", 6)):
        lw[name] = col(i).astype(BF16)
    pad = jnp.zeros((d, LANES - rope - hf), F32)
    lw["mla_in"] = jnp.concatenate([col(7), col(8), col(9), col(3), pad], axis=1).astype(BF16)
    lw["bf"] = jnp.zeros((1, LANES), F32).at[0, rope:rope + hf].set(b_forget[l])
    wq = w_q_up[l].reshape(q_lora, hm, MLA_NOPE + rope)
    wq = jnp.pad(wq, ((0, 0), (0, 0), (0, MLA_QK_PAD - MLA_NOPE - rope)))
    lw["wq"] = wq.reshape(q_lora, hm * MLA_QK_PAD).astype(BF16)
    wkv = w_kv_up[l].reshape(kv_lora, hm, MLA_NOPE + MLA_V)
    lw["wk"] = wkv[:, :, :MLA_NOPE].reshape(kv_lora, hm * MLA_NOPE).astype(BF16)
    lw["wv"] = wkv[:, :, MLA_NOPE:].reshape(kv_lora, hm * MLA_V).astype(BF16)
    lw["w_out"] = w_out[l].astype(BF16)
    for f in range(2):
        lw[f"wg{f}"] = w_ffn_gate[l, f].astype(BF16)
        lw[f"wu{f}"] = w_ffn_up[l, f].astype(BF16)
        lw[f"wd{f}"] = w_ffn_down[l, f].astype(BF16)
    return lw


def _ffn(h, lw, f):
    b, s, d = h.shape
    a = _swiglu(h.reshape(b * s, d), lw[f"wg{f}"], lw[f"wu{f}"])
    (y,) = _mm(a, lw[f"wd{f}"], (F32,), "ffn_down", tm_pref=512, tn_pref=512)
    return y.reshape(b, s, d)


def _token_mix(h, lw, dims, tables, cache):
    hf, hs, hm, q_lora, kv_lora, rope = dims
    b, s, d = h.shape
    h2 = h.reshape(b * s, d)
    cos_t, sin_t = tables
    three = lambda a: a.reshape(b, s, -1)

    qk_scale = HEAD_DIM ** -0.5 * LOG2E
    mla_scale = (MLA_NOPE + rope) ** -0.5 * LOG2E
    (fq,) = _mm(h2, lw["fq"], (BF16,), "in_fox_q", out_scale=qk_scale)
    fk32, fk16 = _mm(h2, lw["fk"], (F32, BF16), "in_fox_k")
    fv32, fv16 = _mm(h2, lw["fv"], (F32, BF16), "in_fox_v")
    (sq,) = _mm(h2, lw["sq"], (BF16,), "in_sb_q", out_scale=qk_scale)
    sk32, sk16 = _mm(h2, lw["sk"], (F32, BF16), "in_sb_k")
    sv32, sv16 = _mm(h2, lw["sv"], (F32, BF16), "in_sb_v")
    (mlat,) = _mm(h2, lw["mla_in"], (F32,), "in_mla", tm_pref=512, tn_pref=lw["mla_in"].shape[1])

    qlat, ckv32, ckv16, krlf = _mla_prep(mlat, lw["gq"], lw["gkv"], lw["bf"], cos_t, sin_t, q_lora, kv_lora, hf)
    q_mla = _q_up(qlat, lw["wq"], cos_t, sin_t, hm, mla_scale)
    k_mla, v_mla = _kv_up(ckv16, lw["wk"], lw["wv"], krlf, hm)

    krlf3 = three(krlf)
    krope = krlf3[:, :, :rope]
    logf = krlf3[:, :, rope:rope + hf]

    if cache is None:
        transposed = lambda a: jnp.swapaxes(three(a), 1, 2)
        _, *parts = _cumsum_rows(krlf3, with_parts=True)
        bias_cols = jnp.stack([p.reshape(b, s, LANES)[:, :, rope:rope + hf] for p in parts], axis=-1)
        n_bias = bias_cols.shape[-1]
        pad_cols = jnp.zeros((b, s, hf, MLA_QK_PAD - HEAD_DIM - n_bias), BF16)
        k_fox = jnp.concatenate([fk16.reshape(b, s, hf, HEAD_DIM), bias_cols, pad_cols], axis=-1)
        q_fox = jnp.concatenate([fq.reshape(b, s, hf, HEAD_DIM), jnp.ones((b, s, hf, n_bias), BF16), pad_cols], axis=-1)
        o_fox = _softmax_attn(transposed(q_fox), k_fox.reshape(b, s, -1), transposed(fv16), hf,
                              MLA_QK_PAD, HEAD_DIM, 0, "fox_attn")
        o_sb = _sb_attn(transposed(sq), three(sk16), transposed(sv16), hs)
        o_mla = _softmax_attn(transposed(q_mla), three(k_mla), transposed(v_mla), hm, MLA_QK_PAD, MLA_V, CHUNK, "mla_attn")
    else:
        c_fk, c_fv, c_logf, c_sk, c_sv, c_ckv, c_krope = cache
        past = c_fk.shape[1]
        flat = lambda a: a.reshape(b, past, -1)
        logf_all = jnp.concatenate([c_logf.astype(F32), logf], axis=1)
        logf_all = jnp.pad(logf_all, ((0, 0), (0, 0), (0, LANES - hf)))
        f_all = _cumsum_rows(logf_all)[:, :, :hf]
        f_bh = jnp.swapaxes(f_all, 1, 2)
        o_fox = _softmax_attn_cached(
            three(fq), flat(c_fk), flat(c_fv), three(fk16), three(fv16), hf, HEAD_DIM, HEAD_DIM, 0,
            f_row_c=f_bh[:, :, None, :past], f_row_n=f_bh[:, :, None, past:])
        o_sb = _sb_attn_cached(three(sq), flat(c_sk), flat(c_sv), three(sk16), three(sv16), hs)
        c_kr = jnp.pad(c_krope.reshape(b * past, rope), ((0, 0), (0, LANES - rope)))
        kc_mla, vc_mla = _kv_up(c_ckv.reshape(b * past, kv_lora), lw["wk"], lw["wv"], c_kr, hm)
        o_mla = _softmax_attn_cached(
            three(q_mla), flat(kc_mla), flat(vc_mla), three(k_mla), three(v_mla), hm, MLA_QK_PAD, MLA_V, CHUNK)

    hmix = _group_norm(o_fox, o_sb, o_mla, lw["g_mix"])
    (mix,) = _mm(hmix.reshape(b * s, d), lw["w_out"], (F32,), "mix_out")
    new_state = (
        fk32.reshape(b, s, hf, HEAD_DIM), fv32.reshape(b, s, hf, HEAD_DIM), logf,
        sk32.reshape(b, s, hs, HEAD_DIM), sv32.reshape(b, s, hs, HEAD_DIM),
        ckv32.reshape(b, s, kv_lora), krope,
    )
    return mix.reshape(b, s, d), new_state


def kernel(x_prompt, x_sample, c_prompt, c_sample, cache_fox_k, cache_fox_v, cache_fox_logf, cache_sb_k, cache_sb_v, cache_mla_ckv, cache_mla_krope, w_ada, b_ada, g_norm, w_in, b_forget, g_q_lat, w_q_up, g_kv_lat, w_kv_up, g_mix_out, w_out, w_ffn_gate, w_ffn_up, w_ffn_down, g_final):
    depth = w_ada.shape[0]
    d = x_prompt.shape[-1]
    nbp, nbs = x_prompt.shape[0], x_sample.shape[0]
    past = cache_fox_k.shape[2]
    hf, hs = cache_fox_k.shape[3], cache_sb_k.shape[3]
    kv_lora, rope, q_lora = cache_mla_ckv.shape[-1], cache_mla_krope.shape[-1], g_q_lat.shape[-1]
    hm = w_q_up.shape[-1] // (MLA_NOPE + rope)
    dims = (hf, hs, hm, q_lora, kv_lora, rope)

    c_all = jnp.concatenate([c_prompt, c_sample], axis=0)
    nb = nbp + nbs
    nb_pad = -(-nb // SUBLANES) * SUBLANES
    c_all = jnp.pad(c_all, ((0, nb_pad - nb), (0, 0)))
    mods = _ada(c_all, w_ada, b_ada).reshape(depth, nb_pad, N_MOD, 1, d)
    groups = {"p": (0, nbp), "s": (nbp, nb)}
    mod = lambda l, grp, i: mods[l, groups[grp][0]:groups[grp][1], i]

    tables = {
        "p": _rope_tables(jnp.arange(x_prompt.shape[1]), rope),
        "s": _rope_tables(past + jnp.arange(x_sample.shape[1]), rope),
    }
    caches = (cache_fox_k, cache_fox_v, cache_fox_logf, cache_sb_k, cache_sb_v, cache_mla_ckv, cache_mla_krope)

    xs = {"p": x_prompt, "s": x_sample}
    pending = {"p": None, "s": None}
    states = {"p": [], "s": []}
    for l in range(depth):
        lw = _prep_layer_weights(l, dims, w_in, b_forget, w_q_up, w_kv_up, w_out, w_ffn_gate, w_ffn_up, w_ffn_down)
        lw["gq"], lw["gkv"], lw["g_mix"] = g_q_lat[l], g_kv_lat[l], g_mix_out[l]
        for grp in ("p", "s"):
            m = lambda i: mod(l, grp, i)
            x, h = _prenorm(xs[grp], g_norm[l, 0], m(1), m(0), res=pending[grp])
            y = _ffn(h, lw, 0)
            x, h = _prenorm(x, g_norm[l, 1], m(4), m(3), res=(y, m(2), 0.5))
            cache = None if grp == "p" else tuple(cc[l] for cc in caches)
            mix, st = _token_mix(h, lw, dims, tables[grp], cache)
            states[grp].append(st)
            x, h = _prenorm(x, g_norm[l, 2], m(7), m(6), res=(mix, m(5), 1.0))
            y = _ffn(h, lw, 1)
            xs[grp] = x
            pending[grp] = (y, m(8), 0.5)

    y_prompt = _prenorm(xs["p"], g_final, res=pending["p"], final=True)
    y_sample = _prenorm(xs["s"], g_final, res=pending["s"], final=True)
    stack = lambda grp, i: jnp.stack([st[i] for st in states[grp]])
    return (y_prompt, y_sample) + tuple(stack("p", i) for i in range(7)) + tuple(stack("s", i) for i in range(7))
```

```python
import functools

import jax
import jax.numpy as jnp
from jax import lax
from jax.experimental import pallas as pl
from jax.experimental.pallas import tpu as pltpu

F32 = jnp.float32
BF16 = jnp.bfloat16

HEAD_DIM = 128
MLA_NOPE = 128
MLA_V = 128
CHUNK = 64
N_MOD = 9
EPS = 1e-6
ROPE_THETA = 10000.0

LANES = 128
SUBLANES = 8
MLA_QK_PAD = 256
VMEM_LIMIT = 56 * 1024 * 1024
LOG2E = 1.4426950408889634

ATTN_TILE = 512
ATTN_HEAD_GROUP = 2
N_BIAS_TERMS = 3
SB_SUB_TILE = 256


def _cparams(n_axes, sequential_axes=()):
    sem = tuple("arbitrary" if a in sequential_axes else "parallel" for a in range(n_axes))
    return pltpu.CompilerParams(dimension_semantics=sem, vmem_limit_bytes=VMEM_LIMIT)


def _tile(n, pref, align):
    if n <= pref:
        return n
    for t in range(pref - pref % align, 0, -align):
        if n % t == 0:
            return t
    return n


def _dot(a, b):
    return jnp.dot(a, b, preferred_element_type=F32)


def _dot_nt(a, b):
    return lax.dot_general(a, b, (((1,), (1,)), ((), ())), preferred_element_type=F32)


def _softplus_neg_abs(z):
    return jnp.log1p(jnp.exp(-jnp.abs(z)))


def _rms(x):
    return x * lax.rsqrt(jnp.mean(x * x, axis=-1, keepdims=True) + EPS)


def _rope_tile(t, cos, sin):
    lane = lax.broadcasted_iota(jnp.int32, t.shape, 1)
    half = LANES // 4
    partner = jnp.where(lane < half, pltpu.roll(t, LANES - half, 1), pltpu.roll(t, half, 1))
    return t * cos + partner * sin


def _ada_kernel(c_ref, w_ref, b_ref, o_ref):
    c = c_ref[...]
    a = (c / (1.0 + jnp.exp(-c))).astype(BF16)
    o_ref[0] = _dot(a, w_ref[0].astype(BF16)) + b_ref[0]


def _ada(c_all, w_ada, b_ada):
    depth, d, n = w_ada.shape
    nb = c_all.shape[0]
    tn = _tile(n, 512, LANES)
    return pl.pallas_call(
        _ada_kernel,
        grid=(depth, n // tn),
        in_specs=[
            pl.BlockSpec((nb, d), lambda l, j: (0, 0)),
            pl.BlockSpec((1, d, tn), lambda l, j: (l, 0, j)),
            pl.BlockSpec((1, 1, tn), lambda l, j: (l, 0, j)),
        ],
        out_specs=pl.BlockSpec((1, nb, tn), lambda l, j: (l, 0, j)),
        out_shape=jax.ShapeDtypeStruct((depth, nb, n), F32),
        compiler_params=_cparams(2),
        name="ada",
    )(c_all, w_ada, b_ada.reshape(depth, 1, n))


def _prenorm_kernel(*refs, coef, has_res, final):
    refs = list(refs)
    x_ref = refs.pop(0)
    x = x_ref[0]
    if has_res:
        y_ref = refs.pop(0)
        gate_ref = refs.pop(0)
        x = x + (coef * gate_ref[0]) * y_ref[0]
    g_ref = refs.pop(0)
    if final:
        (o_ref,) = refs
        o_ref[0] = _rms(x) * g_ref[...]
        return
    sc_ref = refs.pop(0)
    sh_ref = refs.pop(0)
    if has_res:
        xo_ref = refs.pop(0)
        xo_ref[0] = x
    (h_ref,) = refs
    h = (_rms(x) * g_ref[...]) * (1.0 + sc_ref[0]) + sh_ref[0]
    h_ref[0] = h.astype(BF16)


def _prenorm(x, g, scale=None, shift=None, res=None, final=False):
    b, s, d = x.shape
    tr = _tile(s, 256, SUBLANES)
    row = pl.BlockSpec((1, tr, d), lambda bi, i: (bi, i, 0))
    per_b = pl.BlockSpec((1, 1, d), lambda bi, i: (bi, 0, 0))
    shared = pl.BlockSpec((1, d), lambda bi, i: (0, 0))
    args, specs = [x], [row]
    coef = 0.0
    if res is not None:
        y, gate, coef = res
        args += [y, gate]
        specs += [row, per_b]
    args.append(g.reshape(1, d))
    specs.append(shared)
    out_shape, out_specs = [], []
    if final:
        out_shape.append(jax.ShapeDtypeStruct((b, s, d), F32))
        out_specs.append(row)
    else:
        args += [scale, shift]
        specs += [per_b, per_b]
        if res is not None:
            out_shape.append(jax.ShapeDtypeStruct((b, s, d), F32))
            out_specs.append(row)
        out_shape.append(jax.ShapeDtypeStruct((b, s, d), BF16))
        out_specs.append(row)
    outs = pl.pallas_call(
        functools.partial(_prenorm_kernel, coef=coef, has_res=res is not None, final=final),
        grid=(b, s // tr),
        in_specs=specs,
        out_specs=out_specs,
        out_shape=out_shape,
        compiler_params=_cparams(2),
        name="final_norm" if final else "prenorm",
    )(*args)
    if final:
        return outs[0]
    if res is None:
        return x, outs[0]
    return outs[0], outs[1]


def _weight_spec(w, w_index, k, tn, col_block0=0):
    lead = tuple(w_index)
    return pl.BlockSpec((None,) * len(lead) + (k, tn), lambda i, j: lead + (0, col_block0 + j))


def _mm_kernel(a_ref, w_ref, *o_refs, out_scale):
    acc = _dot(a_ref[...].astype(BF16), w_ref[...])
    if out_scale != 1.0:
        acc = acc * out_scale
    for o_ref in o_refs:
        o_ref[...] = acc.astype(o_ref.dtype)


def _mm(a, w, out_dtypes, name, w_index=(), col0=0, n=None, tm_pref=1024, tn_pref=512, out_scale=1.0):
    m, k = a.shape
    n = w.shape[-1] if n is None else n
    tm = _tile(m, tm_pref, SUBLANES)
    tn = _tile(n, tn_pref, LANES)
    assert col0 % tn == 0
    outs = pl.pallas_call(
        functools.partial(_mm_kernel, out_scale=out_scale),
        grid=(m // tm, n // tn),
        in_specs=[pl.BlockSpec((tm, k), lambda i, j: (i, 0)), _weight_spec(w, w_index, k, tn, col0 // tn)],
        out_specs=[pl.BlockSpec((tm, tn), lambda i, j: (i, j)) for _ in out_dtypes],
        out_shape=[jax.ShapeDtypeStruct((m, n), dt) for dt in out_dtypes],
        compiler_params=_cparams(2),
        name=name,
    )(a, w)
    return outs


def _mm_state_kernel(a_ref, w_ref, *refs, n_heads):
    o32_ref, o16_ref = refs[-2:]
    acc = _dot(a_ref[...], w_ref[...])
    o16_ref[...] = acc.astype(BF16)
    hd = acc.shape[1] // n_heads
    for h in range(n_heads):
        o32_ref[:, h, :] = acc[:, h * hd:(h + 1) * hd]


def _mm_state(a, w, w_index, col0, n_heads, layer, stacked, depth, name, tm_pref=512):
    m, k = a.shape
    hd = HEAD_DIM
    n = n_heads * hd
    tm = _tile(m, tm_pref, 2 * SUBLANES)
    assert col0 % n == 0
    in_specs = [pl.BlockSpec((tm, k), lambda i: (i, 0)),
                pl.BlockSpec((None,) * len(w_index) + (k, n), lambda i: tuple(w_index) + (0, col0 // n))]
    args = [a, w]
    aliases = {}
    if stacked is not None:
        in_specs.append(pl.BlockSpec(memory_space=pl.ANY))
        args.append(stacked)
        aliases = {2: 0}
    return pl.pallas_call(
        functools.partial(_mm_state_kernel, n_heads=n_heads),
        grid=(m // tm,),
        in_specs=in_specs,
        out_specs=[pl.BlockSpec((None, tm, n_heads, hd), lambda i: (layer, i, 0, 0)),
                   pl.BlockSpec((tm, n), lambda i: (i, 0))],
        out_shape=[jax.ShapeDtypeStruct((depth, m, n_heads, hd), F32), jax.ShapeDtypeStruct((m, n), BF16)],
        input_output_aliases=aliases,
        compiler_params=_cparams(1),
        name=name,
    )(*args)


def _swiglu_kernel(a_ref, wg_ref, wu_ref, o_ref, *, rows):
    for r0 in range(0, a_ref.shape[0], rows):
        a = a_ref[r0:r0 + rows, :]
        g = _dot(a, wg_ref[...])
        u = _dot(a, wu_ref[...])
        o_ref[r0:r0 + rows, :] = ((g / (1.0 + jnp.exp(-g))) * u).astype(o_ref.dtype)


def _swiglu(a, wg, wu, w_index, tm_pref=2048, tn_pref=256, rows_pref=512):
    m, k = a.shape
    n = wg.shape[-1]
    tm = _tile(m, tm_pref, SUBLANES)
    tn = _tile(n, tn_pref, LANES)
    rows = _tile(tm, rows_pref, SUBLANES)
    wspec = _weight_spec(wg, w_index, k, tn)
    return pl.pallas_call(
        functools.partial(_swiglu_kernel, rows=rows),
        grid=(m // tm, n // tn),
        in_specs=[pl.BlockSpec((tm, k), lambda i, j: (i, 0)), wspec, wspec],
        out_specs=pl.BlockSpec((tm, tn), lambda i, j: (i, j)),
        out_shape=jax.ShapeDtypeStruct((m, n), BF16),
        compiler_params=_cparams(2),
        name="ffn_swiglu",
    )(a, wg, wu)


def _mla_prep_kernel(m_ref, gq_ref, gkv_ref, bf_ref, cos_ref, sin_ref,
                     ql_ref, ckv32_ref, ckv16_ref, kr_ref, *, q_lora, kv_lora, n_forget):
    x = m_ref[...]
    ql_ref[...] = (_rms(x[:, :q_lora]) * gq_ref[...]).astype(BF16)
    ckv = _rms(x[:, q_lora:q_lora + kv_lora]) * gkv_ref[...]
    ckv32_ref[...] = ckv
    ckv16_ref[...] = ckv.astype(BF16)
    t = x[:, q_lora + kv_lora:]
    kr = _rope_tile(t, cos_ref[...], sin_ref[...])
    z = t + bf_ref[...]
    logf = jnp.minimum(z, 0.0) - _softplus_neg_abs(z)
    lane = lax.broadcasted_iota(jnp.int32, t.shape, 1)
    rope_w = LANES // 2
    kr_ref[...] = jnp.where(lane < rope_w, kr, jnp.where(lane < rope_w + n_forget, logf, 0.0))


def _mla_prep(mlat, gq, gkv, bf_tile, cos_t, sin_t, q_lora, kv_lora, n_forget):
    m = mlat.shape[0]
    s = cos_t.shape[0]
    tr = _tile(s, 256, SUBLANES)
    ns = s // tr
    rows = lambda w: pl.BlockSpec((tr, w), lambda i: (i, 0))
    shared = lambda w: pl.BlockSpec((1, w), lambda i: (0, 0))
    table = pl.BlockSpec((tr, LANES), lambda i: (i % ns, 0))
    return pl.pallas_call(
        functools.partial(_mla_prep_kernel, q_lora=q_lora, kv_lora=kv_lora, n_forget=n_forget),
        grid=(m // tr,),
        in_specs=[rows(mlat.shape[1]), shared(q_lora), shared(kv_lora), shared(LANES), table, table],
        out_specs=[rows(q_lora), rows(kv_lora), rows(kv_lora), rows(LANES)],
        out_shape=[
            jax.ShapeDtypeStruct((m, q_lora), BF16),
            jax.ShapeDtypeStruct((m, kv_lora), F32),
            jax.ShapeDtypeStruct((m, kv_lora), BF16),
            jax.ShapeDtypeStruct((m, LANES), F32),
        ],
        compiler_params=_cparams(1),
        name="mla_prep",
    )(mlat, gq.reshape(1, q_lora), gkv.reshape(1, kv_lora), bf_tile, cos_t, sin_t)


def _q_up_kernel(a_ref, w_ref, cos_ref, sin_ref, o_ref, *, n_heads, out_scale):
    acc = _dot(a_ref[...], w_ref[...]) * out_scale
    cos = cos_ref[...]
    sin = sin_ref[...]
    for h in range(n_heads):
        base = h * MLA_QK_PAD
        o_ref[:, base:base + MLA_NOPE] = acc[:, base:base + MLA_NOPE].astype(BF16)
        t = acc[:, base + MLA_NOPE:base + MLA_QK_PAD]
        o_ref[:, base + MLA_NOPE:base + MLA_QK_PAD] = _rope_tile(t, cos, sin).astype(BF16)


def _q_up(qlat, wq, cos_t, sin_t, n_heads, out_scale):
    m, k = qlat.shape
    n = wq.shape[1]
    s = cos_t.shape[0]
    tr = _tile(s, 512, SUBLANES)
    ns = s // tr
    table = pl.BlockSpec((tr, LANES), lambda i: (i % ns, 0))
    return pl.pallas_call(
        functools.partial(_q_up_kernel, n_heads=n_heads, out_scale=out_scale),
        grid=(m // tr,),
        in_specs=[pl.BlockSpec((tr, k), lambda i: (i, 0)), pl.BlockSpec((k, n), lambda i: (0, 0)), table, table],
        out_specs=pl.BlockSpec((tr, n), lambda i: (i, 0)),
        out_shape=jax.ShapeDtypeStruct((m, n), BF16),
        compiler_params=_cparams(1),
        name="mla_q_up",
    )(qlat, wq, cos_t, sin_t)


def _kv_up_kernel(a_ref, wk_ref, wv_ref, kr_ref, k_ref, v_ref, *, n_heads):
    a = a_ref[...].astype(BF16)
    kn = _dot(a, wk_ref[...])
    v_ref[...] = _dot(a, wv_ref[...]).astype(BF16)
    kr = kr_ref[...]
    lane = lax.broadcasted_iota(jnp.int32, kr.shape, 1)
    krm = jnp.where(lane < LANES // 2, kr, 0.0).astype(BF16)
    for h in range(n_heads):
        base = h * MLA_QK_PAD
        k_ref[:, base:base + MLA_NOPE] = kn[:, h * MLA_NOPE:(h + 1) * MLA_NOPE].astype(BF16)
        k_ref[:, base + MLA_NOPE:base + MLA_QK_PAD] = krm


def _kv_up(ckv, wk, wv, kr, n_heads):
    m, k = ckv.shape
    tm = _tile(m, 512, SUBLANES)
    rows = lambda w: pl.BlockSpec((tm, w), lambda i: (i, 0))
    whole = lambda a: pl.BlockSpec(a.shape, lambda i: (0, 0))
    return pl.pallas_call(
        functools.partial(_kv_up_kernel, n_heads=n_heads),
        grid=(m // tm,),
        in_specs=[rows(k), whole(wk), whole(wv), rows(LANES)],
        out_specs=[rows(n_heads * MLA_QK_PAD), rows(n_heads * MLA_V)],
        out_shape=[
            jax.ShapeDtypeStruct((m, n_heads * MLA_QK_PAD), BF16),
            jax.ShapeDtypeStruct((m, n_heads * MLA_V), BF16),
        ],
        compiler_params=_cparams(1),
        name="mla_kv_up",
    )(ckv, wk, wv, kr)


def _split3(x):
    x1 = x.astype(BF16)
    r1 = x - x1.astype(F32)
    x2 = r1.astype(BF16)
    x3 = (r1 - x2.astype(F32)).astype(BF16)
    return x1, x2, x3


def _cumsum_kernel(x_ref, o_ref, *rest, tc, bias_lanes):
    carry_ref = rest[-1]

    @pl.when(pl.program_id(1) == 0)
    def _():
        carry_ref[...] = jnp.zeros_like(carry_ref)

    r = lax.broadcasted_iota(jnp.int32, (tc, tc), 0)
    c = lax.broadcasted_iota(jnp.int32, (tc, tc), 1)
    tri = jnp.where(c <= r, 1.0, 0.0).astype(BF16)
    x1, x2, x3 = _split3(x_ref[0])
    y = (_dot(tri, x3) + _dot(tri, x2)) + _dot(tri, x1) + carry_ref[...]
    o_ref[0] = y
    carry_ref[...] = y[tc - 1:tc, :]
    if bias_lanes is not None:
        kb_ref = rest[0]
        lane0, n_heads = bias_lanes
        parts = [p.astype(F32) for p in _split3(y * (-LOG2E))]
        lane = lax.broadcasted_iota(jnp.int32, (tc, LANES), 1)
        for h in range(n_heads):
            col = [jnp.broadcast_to(p[:, lane0 + h:lane0 + h + 1], (tc, LANES)) for p in parts]
            blk = jnp.where(lane == 0, col[0], jnp.where(lane == 1, col[1], jnp.where(lane == 2, col[2], 0.0)))
            kb_ref[0, :, h * LANES:(h + 1) * LANES] = blk.astype(BF16)


def _cumsum_rows(x, bias_lanes=None):
    b, s, w = x.shape
    tc = _tile(s, 256, 2 * SUBLANES)
    blk = pl.BlockSpec((1, tc, w), lambda bi, j: (bi, j, 0))
    out_specs, out_shape = [blk], [jax.ShapeDtypeStruct((b, s, w), F32)]
    if bias_lanes is not None:
        wide = bias_lanes[1] * LANES
        out_specs.append(pl.BlockSpec((1, tc, wide), lambda bi, j: (bi, j, 0)))
        out_shape.append(jax.ShapeDtypeStruct((b, s, wide), BF16))
    outs = pl.pallas_call(
        functools.partial(_cumsum_kernel, tc=tc, bias_lanes=bias_lanes),
        grid=(b, s // tc),
        in_specs=[blk],
        out_specs=out_specs,
        out_shape=out_shape,
        scratch_shapes=[pltpu.VMEM((1, w), F32)],
        compiler_params=_cparams(2, sequential_axes=(1,)),
        name="cumsum_rows",
    )(x)
    return outs if bias_lanes is not None else outs[0]


def _rep_lanes(x, width):
    return x if width == LANES else jnp.concatenate([x] * (width // LANES), axis=1)


def _softmax_attn_kernel(*refs, t, group, dq, dv, chunk, has_bias):
    if has_bias:
        qt_ref, k_ref, kb_ref, vt_ref, o_ref, m_ref, l_ref, acc_ref = refs
    else:
        qt_ref, k_ref, vt_ref, o_ref, m_ref, l_ref, acc_ref = refs
    i = pl.program_id(2)
    m_ref[...] = jnp.full_like(m_ref, -jnp.inf)
    l_ref[...] = jnp.zeros_like(l_ref)
    acc_ref[...] = jnp.zeros_like(acc_ref)
    if has_bias:
        row = lax.broadcasted_iota(jnp.int32, (LANES, t), 0)
        ones_rows = jnp.where(row < N_BIAS_TERMS, 1.0, 0.0).astype(BF16)

    def score(g, ks):
        k = k_ref[0, pl.ds(ks, t), g * dq:(g + 1) * dq]
        qt = qt_ref[0, g * dq:(g + 1) * dq, :]
        if has_bias:
            k = jnp.concatenate([k, kb_ref[0, pl.ds(ks, t), g * LANES:(g + 1) * LANES]], axis=1)
            qt = jnp.concatenate([qt, ones_rows], axis=0)
        return _dot(k, qt)

    def step(j, masked):
        ks = pl.multiple_of(j * t, t)
        heads = range(group)
        scores = [score(g, ks) for g in heads]
        probs, alphas = [], []
        for g in heads:
            s = scores[g]
            if masked:
                kpos = lax.broadcasted_iota(jnp.int32, s.shape, 0)
                qpos = lax.broadcasted_iota(jnp.int32, s.shape, 1)
                visible = (kpos // chunk) <= (qpos // chunk) if chunk else kpos <= qpos
                s = jnp.where(visible, s, -jnp.inf)
            m_prev = m_ref[g]
            m_new = jnp.maximum(m_prev, jnp.max(s, axis=0, keepdims=True))
            alpha = jnp.exp2(m_prev - m_new)
            p = jnp.exp2(s - m_new)
            l_ref[g] = alpha * l_ref[g] + jnp.sum(p, axis=0, keepdims=True)
            m_ref[g] = m_new
            probs.append(p.astype(BF16))
            alphas.append(alpha)
        for g in heads:
            pv = _dot(vt_ref[0, g * dv:(g + 1) * dv, pl.ds(ks, t)], probs[g])
            acc_ref[g] = alphas[g] * acc_ref[g] + pv

    def body(j, carry):
        step(j, False)
        return carry

    lax.fori_loop(0, i, body, 0)
    step(i, True)
    for g in range(group):
        o_ref[0, :, g * dv:(g + 1) * dv] = (acc_ref[g] / l_ref[g]).T


def _head_group(n_heads):
    return ATTN_HEAD_GROUP if n_heads % ATTN_HEAD_GROUP == 0 else 1


def _softmax_attn(qt, k, vt, n_heads, dq, dv, chunk, name, kb=None):
    b, s, _ = k.shape
    t = _tile(s, ATTN_TILE, LANES)
    grp = _head_group(n_heads)
    stat = pltpu.VMEM((grp, 1, t), F32)
    keys = lambda w: pl.BlockSpec((1, s, grp * w), lambda bi, h, i: (bi, 0, h))
    in_specs = [pl.BlockSpec((1, grp * dq, t), lambda bi, h, i: (bi, h, i)), keys(dq)]
    args = [qt, k]
    if kb is not None:
        in_specs.append(keys(LANES))
        args.append(kb)
    in_specs.append(pl.BlockSpec((1, grp * dv, s), lambda bi, h, i: (bi, h, 0)))
    args.append(vt)
    return pl.pallas_call(
        functools.partial(_softmax_attn_kernel, t=t, group=grp, dq=dq, dv=dv, chunk=chunk, has_bias=kb is not None),
        grid=(b, n_heads // grp, s // t),
        in_specs=in_specs,
        out_specs=pl.BlockSpec((1, t, grp * dv), lambda bi, h, i: (bi, i, h)),
        out_shape=jax.ShapeDtypeStruct((b, s, n_heads * dv), F32),
        scratch_shapes=[stat, stat, pltpu.VMEM((grp, dv, t), F32)],
        compiler_params=_cparams(3),
        name=name,
    )(*args)


def _softmax_attn_cached_kernel(*refs, has_bias, chunk, past):
    if has_bias:
        q_ref, kc_ref, vc_ref, kn_ref, vn_ref, fkc_ref, fkn_ref, o_ref = refs
    else:
        q_ref, kc_ref, vc_ref, kn_ref, vn_ref, o_ref = refs
    q = q_ref[...]
    s_c = _dot_nt(q, kc_ref[...].astype(BF16))
    s_n = _dot_nt(q, kn_ref[...])
    if has_bias:
        s_c = s_c - fkc_ref[...] * LOG2E
        s_n = s_n - fkn_ref[...] * LOG2E
    r = lax.broadcasted_iota(jnp.int32, s_n.shape, 0) + past
    c = lax.broadcasted_iota(jnp.int32, s_n.shape, 1) + past
    visible = (c // chunk) <= (r // chunk) if chunk else c <= r
    s_n = jnp.where(visible, s_n, -jnp.inf)
    m = jnp.maximum(jnp.max(s_c, axis=-1, keepdims=True), jnp.max(s_n, axis=-1, keepdims=True))
    p_c = jnp.exp2(s_c - m)
    p_n = jnp.exp2(s_n - m)
    l = jnp.sum(p_c, axis=-1, keepdims=True) + jnp.sum(p_n, axis=-1, keepdims=True)
    acc = _dot(p_c.astype(BF16), vc_ref[...].astype(BF16)) + _dot(p_n.astype(BF16), vn_ref[...])
    o_ref[...] = acc / l


def _cache_spec(cache, layer, d):
    if layer is None:
        return pl.BlockSpec((None, cache.shape[1], d), lambda bi, h: (bi, 0, h))
    return pl.BlockSpec((None, None, cache.shape[2], d), lambda bi, h: (layer, bi, 0, h))


def _softmax_attn_cached(q, kc, vc, kn, vn, n_heads, dq, dv, chunk, f_row_c=None, f_row_n=None, layer=None):
    b, sn, _ = q.shape
    past = kc.shape[-2]
    has_bias = f_row_c is not None
    new = lambda d: pl.BlockSpec((None, sn, d), lambda bi, h: (bi, 0, h))
    in_specs = [new(dq), _cache_spec(kc, layer, dq), _cache_spec(vc, layer, dv), new(dq), new(dv)]
    args = [q, kc, vc, kn, vn]
    if has_bias:
        in_specs += [
            pl.BlockSpec((None, None, 1, past), lambda bi, h: (bi, h, 0, 0)),
            pl.BlockSpec((None, None, 1, sn), lambda bi, h: (bi, h, 0, 0)),
        ]
        args += [f_row_c, f_row_n]
    return pl.pallas_call(
        functools.partial(_softmax_attn_cached_kernel, has_bias=has_bias, chunk=chunk, past=past),
        grid=(b, n_heads),
        in_specs=in_specs,
        out_specs=new(dv),
        out_shape=jax.ShapeDtypeStruct((b, sn, n_heads * dv), F32),
        compiler_params=_cparams(2),
        name="fox_attn_cached" if has_bias else "mla_attn_cached",
    )(*args)


def _suffix_matrix(t):
    r = lax.broadcasted_iota(jnp.int32, (t, t), 0)
    c = lax.broadcasted_iota(jnp.int32, (t, t), 1)
    return jnp.where(r >= c, 1.0, 0.0).astype(BF16)


def _sb_fold(q, k, v, u, c, strict):
    z = _dot_nt(q, k)
    neg_abs = lax.bitcast_convert_type(lax.bitcast_convert_type(z, jnp.uint32) | jnp.uint32(0x80000000), F32)
    sp = jnp.maximum(z, 0.0) + jnp.log(1.0 + jnp.exp2(neg_abs)) * LOG2E
    if strict is not None:
        sp = jnp.where(strict, sp, 0.0)
    hi = lax.bitcast_convert_type(lax.bitcast_convert_type(sp, jnp.uint32) & jnp.uint32(0xFFFF0000), F32)
    lo = (sp - hi).astype(BF16)
    suffix = _dot(lo, u) + _dot(hi.astype(BF16), u)
    width = z.shape[1]
    c_wide = c[:, :width] if width < LANES else _rep_lanes(c, width)
    a = jnp.exp2(z - (suffix + c_wide))
    if strict is not None:
        a = jnp.where(strict, a, 0.0)
    return _dot(a.astype(BF16), v), c + jnp.sum(sp, axis=-1, keepdims=True)


def _sb_attn_kernel(qt_ref, k_ref, vt_ref, o_ref, c_ref, acc_ref, *, t, tsub, group, d):
    i = pl.program_id(2)
    r = lax.broadcasted_iota(jnp.int32, (tsub, tsub), 0)
    col = lax.broadcasted_iota(jnp.int32, (tsub, tsub), 1)
    ut = jnp.where(col >= r, 1.0, 0.0).astype(BF16)
    c_ref[...] = jnp.zeros_like(c_ref)
    acc_ref[...] = jnp.zeros_like(acc_ref)

    def step(j, masked):
        chains = [(g, sub) for g in range(group) for sub in reversed(range(t // tsub))]
        start = {sub: pl.multiple_of(j * t + sub * tsub, tsub) for _, sub in chains}
        z, sp, strict, suffix = {}, {}, {}, {}
        for ch in chains:
            g, sub = ch
            z[ch] = _dot(k_ref[0, pl.ds(start[sub], tsub), g * d:(g + 1) * d], qt_ref[0, g * d:(g + 1) * d, :])
        for ch in chains:
            g, sub = ch
            neg_abs = lax.bitcast_convert_type(
                lax.bitcast_convert_type(z[ch], jnp.uint32) | jnp.uint32(0x80000000), F32)
            sp[ch] = jnp.maximum(z[ch], 0.0) + jnp.log(1.0 + jnp.exp2(neg_abs)) * LOG2E
            if masked:
                kpos = lax.broadcasted_iota(jnp.int32, (tsub, t), 0) + sub * tsub
                qpos = lax.broadcasted_iota(jnp.int32, (tsub, t), 1)
                strict[ch] = kpos < qpos
                sp[ch] = jnp.where(strict[ch], sp[ch], 0.0)
            hi = lax.bitcast_convert_type(
                lax.bitcast_convert_type(sp[ch], jnp.uint32) & jnp.uint32(0xFFFF0000), F32)
            lo = (sp[ch] - hi).astype(BF16)
            suffix[ch] = _dot(ut, lo) + _dot(ut, hi.astype(BF16))
        c = {g: c_ref[g] for g in range(group)}
        acc = {g: acc_ref[g] for g in range(group)}
        for ch in chains:
            g, sub = ch
            a = jnp.exp2(z[ch] - (suffix[ch] + c[g]))
            if masked:
                a = jnp.where(strict[ch], a, 0.0)
            acc[g] = acc[g] + _dot(vt_ref[0, g * d:(g + 1) * d, pl.ds(start[sub], tsub)], a.astype(BF16))
            c[g] = c[g] + suffix[ch][0:1, :]
        for g in range(group):
            c_ref[g] = c[g]
            acc_ref[g] = acc[g]

    step(i, True)

    def body(jj, carry):
        step(i - 1 - jj, False)
        return carry

    lax.fori_loop(0, i, body, 0)
    for g in range(group):
        o_ref[0, :, g * d:(g + 1) * d] = acc_ref[g].T


def _sb_attn(qt, k, vt, n_heads):
    b, s, _ = k.shape
    d = HEAD_DIM
    t = _tile(s, ATTN_TILE, LANES)
    tsub = _tile(t, SB_SUB_TILE, LANES)
    grp = _head_group(n_heads)
    return pl.pallas_call(
        functools.partial(_sb_attn_kernel, t=t, tsub=tsub, group=grp, d=d),
        grid=(b, n_heads // grp, s // t),
        in_specs=[
            pl.BlockSpec((1, grp * d, t), lambda bi, h, i: (bi, h, i)),
            pl.BlockSpec((1, s, grp * d), lambda bi, h, i: (bi, 0, h)),
            pl.BlockSpec((1, grp * d, s), lambda bi, h, i: (bi, h, 0)),
        ],
        out_specs=pl.BlockSpec((1, t, grp * d), lambda bi, h, i: (bi, i, h)),
        out_shape=jax.ShapeDtypeStruct((b, s, n_heads * d), F32),
        scratch_shapes=[pltpu.VMEM((grp, 1, t), F32), pltpu.VMEM((grp, d, t), F32)],
        compiler_params=_cparams(3),
        name="sb_attn",
    )(qt, k, vt)


def _sb_attn_cached_kernel(q_ref, kc_ref, vc_ref, kn_ref, vn_ref, o_ref, *, tk):
    q = q_ref[...]
    sn = q.shape[0]
    r = lax.broadcasted_iota(jnp.int32, (sn, sn), 0)
    col = lax.broadcasted_iota(jnp.int32, (sn, sn), 1)
    acc, c = _sb_fold(q, kn_ref[...], vn_ref[...], _suffix_matrix(sn), jnp.zeros((sn, LANES), F32), col < r)
    u = _suffix_matrix(tk)
    for blk in reversed(range(kc_ref.shape[0] // tk)):
        k = kc_ref[blk * tk:(blk + 1) * tk, :].astype(BF16)
        v = vc_ref[blk * tk:(blk + 1) * tk, :].astype(BF16)
        pv, c = _sb_fold(q, k, v, u, c, None)
        acc = acc + pv
    o_ref[...] = acc


def _sb_attn_cached(q, kc, vc, kn, vn, n_heads, layer=None):
    b, sn, _ = q.shape
    d = HEAD_DIM
    tk = _tile(kc.shape[-2], SB_SUB_TILE, LANES)
    new = pl.BlockSpec((None, sn, d), lambda bi, h: (bi, 0, h))
    old = _cache_spec(kc, layer, d)
    return pl.pallas_call(
        functools.partial(_sb_attn_cached_kernel, tk=tk),
        grid=(b, n_heads),
        in_specs=[new, old, old, new, new],
        out_specs=new,
        out_shape=jax.ShapeDtypeStruct((b, sn, n_heads * d), F32),
        compiler_params=_cparams(2),
        name="sb_attn_cached",
    )(q, kc, vc, kn, vn)


def _group_norm_kernel(of_ref, os_ref, om_ref, g_ref, h_ref):
    off = 0
    for ref in (of_ref, os_ref, om_ref):
        w = ref.shape[-1]
        h_ref[0, :, off:off + w] = (_rms(ref[0]) * g_ref[:, off:off + w]).astype(BF16)
        off += w


def _group_norm(o_fox, o_sb, o_mla, g):
    b, s, _ = o_fox.shape
    d = g.shape[-1]
    tr = _tile(s, 256, SUBLANES)
    rows = lambda a: pl.BlockSpec((1, tr, a.shape[-1]), lambda bi, i: (bi, i, 0))
    return pl.pallas_call(
        _group_norm_kernel,
        grid=(b, s // tr),
        in_specs=[rows(o_fox), rows(o_sb), rows(o_mla), pl.BlockSpec((1, d), lambda bi, i: (0, 0))],
        out_specs=pl.BlockSpec((1, tr, d), lambda bi, i: (bi, i, 0)),
        out_shape=jax.ShapeDtypeStruct((b, s, d), BF16),
        compiler_params=_cparams(2),
        name="group_norm",
    )(o_fox, o_sb, o_mla, g.reshape(1, d))


def _rope_tables(pos, rope_dim):
    half = rope_dim // 2
    inv = ROPE_THETA ** (-jnp.arange(half, dtype=F32) / half)
    ang = pos.astype(F32)[:, None] * inv[None, :]
    cos, sin = jnp.cos(ang), jnp.sin(ang)
    n = pos.shape[0]
    pad = LANES - rope_dim
    cos_t = jnp.concatenate([cos, cos, jnp.ones((n, pad), F32)], axis=-1)
    sin_t = jnp.concatenate([-sin, sin, jnp.zeros((n, pad), F32)], axis=-1)
    return cos_t, sin_t


def _prep_layer_weights(l, dims, w_in, b_forget, w_q_up, w_kv_up):
    hf, hs, hm, q_lora, kv_lora, rope = dims
    wf, ws = hf * HEAD_DIM, hs * HEAD_DIM
    d = w_in.shape[1]
    w = w_in[l]
    widths = (wf, wf, wf, hf, ws, ws, ws, q_lora, kv_lora, rope)
    offs = [0]
    for x in widths:
        offs.append(offs[-1] + x)
    col = lambda i: w[:, offs[i]:offs[i + 1]]
    lw = {"fox_cols": (offs[0], offs[1], offs[2])}
    for name, i in (("sq", 4), ("sk", 5), ("sv", 6)):
        lw[name] = col(i).astype(BF16)
    pad = jnp.zeros((d, LANES - rope - hf), F32)
    lw["mla_in"] = jnp.concatenate([col(7), col(8), col(9), col(3), pad], axis=1).astype(BF16)
    lw["bf"] = jnp.zeros((1, LANES), F32).at[0, rope:rope + hf].set(b_forget[l])
    wq = w_q_up[l].reshape(q_lora, hm, MLA_NOPE + rope)
    wq = jnp.pad(wq, ((0, 0), (0, 0), (0, MLA_QK_PAD - MLA_NOPE - rope)))
    lw["wq"] = wq.reshape(q_lora, hm * MLA_QK_PAD).astype(BF16)
    wkv = w_kv_up[l].reshape(kv_lora, hm, MLA_NOPE + MLA_V)
    lw["wk"] = wkv[:, :, :MLA_NOPE].reshape(kv_lora, hm * MLA_NOPE).astype(BF16)
    lw["wv"] = wkv[:, :, MLA_NOPE:].reshape(kv_lora, hm * MLA_V).astype(BF16)
    return lw


def _ffn(h, gw, l, f):
    b, s, d = h.shape
    a = _swiglu(h.reshape(b * s, d), gw["wg"], gw["wu"], (l, f))
    (y,) = _mm(a, gw["wd"], (F32,), "ffn_down", w_index=(l, f), tm_pref=512, tn_pref=512)
    return y.reshape(b, s, d)


def _token_mix(h, l, lw, gw, dims, tables, cache, stacked):
    hf, hs, hm, q_lora, kv_lora, rope = dims
    b, s, d = h.shape
    h2 = h.reshape(b * s, d)
    cos_t, sin_t = tables
    three = lambda a: a.reshape(b, s, -1)
    depth = gw["w_in"].shape[0]

    qk_scale = HEAD_DIM ** -0.5 * LOG2E
    mla_scale = (MLA_NOPE + rope) ** -0.5 * LOG2E
    fq_col, fk_col, fv_col = lw["fox_cols"]
    wf = hf * HEAD_DIM
    stacked = dict(stacked)
    (fq,) = _mm(h2, gw["w_in"], (BF16,), "in_fox_q", w_index=(l,), col0=fq_col, n=wf, out_scale=qk_scale)
    stacked["fk"], fk16 = _mm_state(h2, gw["w_in"], (l,), fk_col, hf, l, stacked["fk"], depth, "in_fox_k")
    stacked["fv"], fv16 = _mm_state(h2, gw["w_in"], (l,), fv_col, hf, l, stacked["fv"], depth, "in_fox_v")
    (sq,) = _mm(h2, lw["sq"], (BF16,), "in_sb_q", out_scale=qk_scale)
    stacked["sk"], sk16 = _mm_state(h2, lw["sk"], (), 0, hs, l, stacked["sk"], depth, "in_sb_k")
    stacked["sv"], sv16 = _mm_state(h2, lw["sv"], (), 0, hs, l, stacked["sv"], depth, "in_sb_v")
    (mlat,) = _mm(h2, lw["mla_in"], (F32,), "in_mla", tm_pref=512, tn_pref=lw["mla_in"].shape[1])

    qlat, ckv32, ckv16, krlf = _mla_prep(mlat, lw["gq"], lw["gkv"], lw["bf"], cos_t, sin_t, q_lora, kv_lora, hf)
    q_mla = _q_up(qlat, lw["wq"], cos_t, sin_t, hm, mla_scale)
    k_mla, v_mla = _kv_up(ckv16, lw["wk"], lw["wv"], krlf, hm)

    krlf3 = three(krlf)
    krope = krlf3[:, :, :rope]
    logf = krlf3[:, :, rope:rope + hf]

    if cache is None:
        transposed = lambda a: jnp.swapaxes(three(a), 1, 2)
        _, k_bias = _cumsum_rows(krlf3, bias_lanes=(rope, hf))
        o_fox = _softmax_attn(transposed(fq), three(fk16), transposed(fv16), hf, HEAD_DIM, HEAD_DIM, 0,
                              "fox_attn", kb=k_bias)
        o_sb = _sb_attn(transposed(sq), three(sk16), transposed(sv16), hs)
        o_mla = _softmax_attn(transposed(q_mla), three(k_mla), transposed(v_mla), hm, MLA_QK_PAD, MLA_V, CHUNK, "mla_attn")
    else:
        c_fk, c_fv, c_logf, c_sk, c_sv, c_ckv, c_krope = cache
        past = c_fk.shape[2]
        flat = lambda a: a.reshape(b, past, -1)
        flat_all = lambda a: a.reshape(a.shape[:3] + (-1,))
        logf_all = jnp.concatenate([c_logf[l].astype(F32), logf], axis=1)
        logf_all = jnp.pad(logf_all, ((0, 0), (0, 0), (0, LANES - hf)))
        f_all = _cumsum_rows(logf_all)[:, :, :hf]
        f_bh = jnp.swapaxes(f_all, 1, 2)
        o_fox = _softmax_attn_cached(
            three(fq), flat_all(c_fk), flat_all(c_fv), three(fk16), three(fv16), hf, HEAD_DIM, HEAD_DIM, 0,
            f_row_c=f_bh[:, :, None, :past], f_row_n=f_bh[:, :, None, past:], layer=l)
        o_sb = _sb_attn_cached(three(sq), flat_all(c_sk), flat_all(c_sv), three(sk16), three(sv16), hs, layer=l)
        c_kr = jnp.pad(c_krope[l].reshape(b * past, rope), ((0, 0), (0, LANES - rope)))
        kc_mla, vc_mla = _kv_up(c_ckv[l].reshape(b * past, kv_lora), lw["wk"], lw["wv"], c_kr, hm)
        o_mla = _softmax_attn_cached(
            three(q_mla), flat(kc_mla), flat(vc_mla), three(k_mla), three(v_mla), hm, MLA_QK_PAD, MLA_V, CHUNK)

    hmix = _group_norm(o_fox, o_sb, o_mla, lw["g_mix"])
    (mix,) = _mm(hmix.reshape(b * s, d), gw["w_out"], (F32,), "mix_out", w_index=(l,))
    return mix.reshape(b, s, d), (logf, ckv32.reshape(b, s, kv_lora), krope), stacked


def kernel(x_prompt, x_sample, c_prompt, c_sample, cache_fox_k, cache_fox_v, cache_fox_logf, cache_sb_k, cache_sb_v, cache_mla_ckv, cache_mla_krope, w_ada, b_ada, g_norm, w_in, b_forget, g_q_lat, w_q_up, g_kv_lat, w_kv_up, g_mix_out, w_out, w_ffn_gate, w_ffn_up, w_ffn_down, g_final):
    depth = w_ada.shape[0]
    d = x_prompt.shape[-1]
    nbp, nbs = x_prompt.shape[0], x_sample.shape[0]
    past = cache_fox_k.shape[2]
    hf, hs = cache_fox_k.shape[3], cache_sb_k.shape[3]
    kv_lora, rope, q_lora = cache_mla_ckv.shape[-1], cache_mla_krope.shape[-1], g_q_lat.shape[-1]
    hm = w_q_up.shape[-1] // (MLA_NOPE + rope)
    dims = (hf, hs, hm, q_lora, kv_lora, rope)

    c_all = jnp.concatenate([c_prompt, c_sample], axis=0)
    nb = nbp + nbs
    nb_pad = -(-nb // SUBLANES) * SUBLANES
    c_all = jnp.pad(c_all, ((0, nb_pad - nb), (0, 0)))
    mods = _ada(c_all, w_ada, b_ada).reshape(depth, nb_pad, N_MOD, 1, d)
    groups = {"p": (0, nbp), "s": (nbp, nb)}
    mod = lambda l, grp, i: mods[l, groups[grp][0]:groups[grp][1], i]

    tables = {
        "p": _rope_tables(jnp.arange(x_prompt.shape[1]), rope),
        "s": _rope_tables(past + jnp.arange(x_sample.shape[1]), rope),
    }
    caches = (cache_fox_k, cache_fox_v, cache_fox_logf, cache_sb_k, cache_sb_v, cache_mla_ckv, cache_mla_krope)

    gw = {"w_in": w_in.astype(BF16), "w_out": w_out.astype(BF16), "wg": w_ffn_gate.astype(BF16),
          "wu": w_ffn_up.astype(BF16), "wd": w_ffn_down.astype(BF16)}

    xs = {"p": x_prompt, "s": x_sample}
    pending = {"p": None, "s": None}
    small = {"p": [], "s": []}
    stacked = {grp: {"fk": None, "fv": None, "sk": None, "sv": None} for grp in ("p", "s")}
    for l in range(depth):
        lw = _prep_layer_weights(l, dims, w_in, b_forget, w_q_up, w_kv_up)
        lw["gq"], lw["gkv"], lw["g_mix"] = g_q_lat[l], g_kv_lat[l], g_mix_out[l]
        for grp in ("p", "s"):
            m = lambda i: mod(l, grp, i)
            x, h = _prenorm(xs[grp], g_norm[l, 0], m(1), m(0), res=pending[grp])
            y = _ffn(h, gw, l, 0)
            x, h = _prenorm(x, g_norm[l, 1], m(4), m(3), res=(y, m(2), 0.5))
            cache = None if grp == "p" else caches
            mix, st, stacked[grp] = _token_mix(h, l, lw, gw, dims, tables[grp], cache, stacked[grp])
            small[grp].append(st)
            x, h = _prenorm(x, g_norm[l, 2], m(7), m(6), res=(mix, m(5), 1.0))
            y = _ffn(h, gw, l, 1)
            xs[grp] = x
            pending[grp] = (y, m(8), 0.5)

    y_prompt = _prenorm(xs["p"], g_final, res=pending["p"], final=True)
    y_sample = _prenorm(xs["s"], g_final, res=pending["s"], final=True)

    def states(grp, x):
        bsz, seq = x.shape[0], x.shape[1]
        big = lambda name: stacked[grp][name].reshape((depth, bsz, seq) + stacked[grp][name].shape[2:])
        stack = lambda i: jnp.stack([st[i] for st in small[grp]])
        return (big("fk"), big("fv"), stack(0), big("sk"), big("sv"), stack(1), stack(2))

    return (y_prompt, y_sample) + states("p", x_prompt) + states("s", x_sample)
```

```python
import functools

import jax
import jax.numpy as jnp
from jax import lax
from jax.experimental import pallas as pl
from jax.experimental.pallas import tpu as pltpu

F32 = jnp.float32
BF16 = jnp.bfloat16

HEAD_DIM = 128
MLA_NOPE = 128
MLA_V = 128
CHUNK = 64
N_MOD = 9
EPS = 1e-6
ROPE_THETA = 10000.0

LANES = 128
SUBLANES = 8
MLA_QK_PAD = 256
VMEM_LIMIT = 56 * 1024 * 1024
LOG2E = 1.4426950408889634

ATTN_TILE = 512
ATTN_HEAD_GROUP = 2
N_BIAS_TERMS = 3
SB_SUB_TILE = 256


def _cparams(n_axes, sequential_axes=()):
    sem = tuple("arbitrary" if a in sequential_axes else "parallel" for a in range(n_axes))
    return pltpu.CompilerParams(dimension_semantics=sem, vmem_limit_bytes=VMEM_LIMIT)


def _tile(n, pref, align):
    if n <= pref:
        return n
    for t in range(pref - pref % align, 0, -align):
        if n % t == 0:
            return t
    return n


def _dot(a, b):
    return jnp.dot(a, b, preferred_element_type=F32)


def _dot_nt(a, b):
    return lax.dot_general(a, b, (((1,), (1,)), ((), ())), preferred_element_type=F32)


def _softplus_neg_abs(z):
    return jnp.log1p(jnp.exp(-jnp.abs(z)))


def _rms(x):
    return x * lax.rsqrt(jnp.mean(x * x, axis=-1, keepdims=True) + EPS)


def _rope_tile(t, cos, sin):
    lane = lax.broadcasted_iota(jnp.int32, t.shape, 1)
    half = LANES // 4
    partner = jnp.where(lane < half, pltpu.roll(t, LANES - half, 1), pltpu.roll(t, half, 1))
    return t * cos + partner * sin


def _ada_kernel(c_ref, w_ref, b_ref, o_ref):
    c = c_ref[...]
    a = (c / (1.0 + jnp.exp(-c))).astype(BF16)
    o_ref[0] = _dot(a, w_ref[0].astype(BF16)) + b_ref[0]


def _ada(c_all, w_ada, b_ada):
    depth, d, n = w_ada.shape
    nb = c_all.shape[0]
    tn = _tile(n, 512, LANES)
    return pl.pallas_call(
        _ada_kernel,
        grid=(depth, n // tn),
        in_specs=[
            pl.BlockSpec((nb, d), lambda l, j: (0, 0)),
            pl.BlockSpec((1, d, tn), lambda l, j: (l, 0, j)),
            pl.BlockSpec((1, 1, tn), lambda l, j: (l, 0, j)),
        ],
        out_specs=pl.BlockSpec((1, nb, tn), lambda l, j: (l, 0, j)),
        out_shape=jax.ShapeDtypeStruct((depth, nb, n), F32),
        compiler_params=_cparams(2),
        name="ada",
    )(c_all, w_ada, b_ada.reshape(depth, 1, n))


def _prenorm_kernel(*refs, coef, has_res, final):
    refs = list(refs)
    x_ref = refs.pop(0)
    x = x_ref[0]
    if has_res:
        y_ref = refs.pop(0)
        gate_ref = refs.pop(0)
        x = x + (coef * gate_ref[0]) * y_ref[0]
    g_ref = refs.pop(0)
    if final:
        (o_ref,) = refs
        o_ref[0] = _rms(x) * g_ref[...]
        return
    sc_ref = refs.pop(0)
    sh_ref = refs.pop(0)
    if has_res:
        xo_ref = refs.pop(0)
        xo_ref[0] = x
    (h_ref,) = refs
    h = (_rms(x) * g_ref[...]) * (1.0 + sc_ref[0]) + sh_ref[0]
    h_ref[0] = h.astype(BF16)


def _prenorm(x, g, scale=None, shift=None, res=None, final=False):
    b, s, d = x.shape
    tr = _tile(s, 256, SUBLANES)
    row = pl.BlockSpec((1, tr, d), lambda bi, i: (bi, i, 0))
    per_b = pl.BlockSpec((1, 1, d), lambda bi, i: (bi, 0, 0))
    shared = pl.BlockSpec((1, d), lambda bi, i: (0, 0))
    args, specs = [x], [row]
    coef = 0.0
    if res is not None:
        y, gate, coef = res
        args += [y, gate]
        specs += [row, per_b]
    args.append(g.reshape(1, d))
    specs.append(shared)
    out_shape, out_specs = [], []
    if final:
        out_shape.append(jax.ShapeDtypeStruct((b, s, d), F32))
        out_specs.append(row)
    else:
        args += [scale, shift]
        specs += [per_b, per_b]
        if res is not None:
            out_shape.append(jax.ShapeDtypeStruct((b, s, d), F32))
            out_specs.append(row)
        out_shape.append(jax.ShapeDtypeStruct((b, s, d), BF16))
        out_specs.append(row)
    outs = pl.pallas_call(
        functools.partial(_prenorm_kernel, coef=coef, has_res=res is not None, final=final),
        grid=(b, s // tr),
        in_specs=specs,
        out_specs=out_specs,
        out_shape=out_shape,
        compiler_params=_cparams(2),
        name="final_norm" if final else "prenorm",
    )(*args)
    if final:
        return outs[0]
    if res is None:
        return x, outs[0]
    return outs[0], outs[1]


def _weight_spec(w, w_index, k, tn, col_block0=0):
    lead = tuple(w_index)
    return pl.BlockSpec((None,) * len(lead) + (k, tn), lambda i, j: lead + (0, col_block0 + j))


def _mm_kernel(a_ref, w_ref, *o_refs, out_scale):
    acc = _dot(a_ref[...].astype(BF16), w_ref[...])
    if out_scale != 1.0:
        acc = acc * out_scale
    for o_ref in o_refs:
        o_ref[...] = acc.astype(o_ref.dtype)


def _mm(a, w, out_dtypes, name, w_index=(), col0=0, n=None, tm_pref=1024, tn_pref=512, out_scale=1.0):
    m, k = a.shape
    n = w.shape[-1] if n is None else n
    tm = _tile(m, tm_pref, SUBLANES)
    tn = _tile(n, tn_pref, LANES)
    assert col0 % tn == 0
    outs = pl.pallas_call(
        functools.partial(_mm_kernel, out_scale=out_scale),
        grid=(m // tm, n // tn),
        in_specs=[pl.BlockSpec((tm, k), lambda i, j: (i, 0)), _weight_spec(w, w_index, k, tn, col0 // tn)],
        out_specs=[pl.BlockSpec((tm, tn), lambda i, j: (i, j)) for _ in out_dtypes],
        out_shape=[jax.ShapeDtypeStruct((m, n), dt) for dt in out_dtypes],
        compiler_params=_cparams(2),
        name=name,
    )(a, w)
    return outs


def _mm_state_kernel(a_ref, w_ref, *refs, n_heads):
    o32_ref, o16_ref = refs[-2:]
    acc = _dot(a_ref[...], w_ref[...])
    o16_ref[...] = acc.astype(BF16)
    hd = acc.shape[1] // n_heads
    for h in range(n_heads):
        o32_ref[:, h, :] = acc[:, h * hd:(h + 1) * hd]


def _mm_state(a, w, w_index, col0, n_heads, layer, stacked, depth, name, tm_pref=512):
    m, k = a.shape
    hd = HEAD_DIM
    n = n_heads * hd
    tm = _tile(m, tm_pref, 2 * SUBLANES)
    assert col0 % n == 0
    in_specs = [pl.BlockSpec((tm, k), lambda i: (i, 0)),
                pl.BlockSpec((None,) * len(w_index) + (k, n), lambda i: tuple(w_index) + (0, col0 // n))]
    args = [a, w]
    aliases = {}
    if stacked is not None:
        in_specs.append(pl.BlockSpec(memory_space=pl.ANY))
        args.append(stacked)
        aliases = {2: 0}
    return pl.pallas_call(
        functools.partial(_mm_state_kernel, n_heads=n_heads),
        grid=(m // tm,),
        in_specs=in_specs,
        out_specs=[pl.BlockSpec((None, tm, n_heads, hd), lambda i: (layer, i, 0, 0)),
                   pl.BlockSpec((tm, n), lambda i: (i, 0))],
        out_shape=[jax.ShapeDtypeStruct((depth, m, n_heads, hd), F32), jax.ShapeDtypeStruct((m, n), BF16)],
        input_output_aliases=aliases,
        compiler_params=_cparams(1),
        name=name,
    )(*args)


def _swiglu_kernel(a_ref, wg_ref, wu_ref, o_ref, *, rows):
    for r0 in range(0, a_ref.shape[0], rows):
        a = a_ref[r0:r0 + rows, :]
        g = _dot(a, wg_ref[...])
        u = _dot(a, wu_ref[...])
        o_ref[r0:r0 + rows, :] = ((g / (1.0 + jnp.exp(-g))) * u).astype(o_ref.dtype)


def _swiglu(a, wg, wu, w_index, tm_pref=2048, tn_pref=256, rows_pref=512):
    m, k = a.shape
    n = wg.shape[-1]
    tm = _tile(m, tm_pref, SUBLANES)
    tn = _tile(n, tn_pref, LANES)
    rows = _tile(tm, rows_pref, SUBLANES)
    wspec = _weight_spec(wg, w_index, k, tn)
    return pl.pallas_call(
        functools.partial(_swiglu_kernel, rows=rows),
        grid=(m // tm, n // tn),
        in_specs=[pl.BlockSpec((tm, k), lambda i, j: (i, 0)), wspec, wspec],
        out_specs=pl.BlockSpec((tm, tn), lambda i, j: (i, j)),
        out_shape=jax.ShapeDtypeStruct((m, n), BF16),
        compiler_params=_cparams(2),
        name="ffn_swiglu",
    )(a, wg, wu)


def _mla_prep_kernel(m_ref, gq_ref, gkv_ref, bf_ref, cos_ref, sin_ref,
                     ql_ref, ckv32_ref, ckv16_ref, kr_ref, *, q_lora, kv_lora, n_forget):
    x = m_ref[...]
    ql_ref[...] = (_rms(x[:, :q_lora]) * gq_ref[...]).astype(BF16)
    ckv = _rms(x[:, q_lora:q_lora + kv_lora]) * gkv_ref[...]
    ckv32_ref[...] = ckv
    ckv16_ref[...] = ckv.astype(BF16)
    t = x[:, q_lora + kv_lora:]
    kr = _rope_tile(t, cos_ref[...], sin_ref[...])
    z = t + bf_ref[...]
    logf = jnp.minimum(z, 0.0) - _softplus_neg_abs(z)
    lane = lax.broadcasted_iota(jnp.int32, t.shape, 1)
    rope_w = LANES // 2
    kr_ref[...] = jnp.where(lane < rope_w, kr, jnp.where(lane < rope_w + n_forget, logf, 0.0))


def _mla_prep(mlat, gq, gkv, bf_tile, cos_t, sin_t, q_lora, kv_lora, n_forget):
    m = mlat.shape[0]
    s = cos_t.shape[0]
    tr = _tile(s, 256, SUBLANES)
    ns = s // tr
    rows = lambda w: pl.BlockSpec((tr, w), lambda i: (i, 0))
    shared = lambda w: pl.BlockSpec((1, w), lambda i: (0, 0))
    table = pl.BlockSpec((tr, LANES), lambda i: (i % ns, 0))
    return pl.pallas_call(
        functools.partial(_mla_prep_kernel, q_lora=q_lora, kv_lora=kv_lora, n_forget=n_forget),
        grid=(m // tr,),
        in_specs=[rows(mlat.shape[1]), shared(q_lora), shared(kv_lora), shared(LANES), table, table],
        out_specs=[rows(q_lora), rows(kv_lora), rows(kv_lora), rows(LANES)],
        out_shape=[
            jax.ShapeDtypeStruct((m, q_lora), BF16),
            jax.ShapeDtypeStruct((m, kv_lora), F32),
            jax.ShapeDtypeStruct((m, kv_lora), BF16),
            jax.ShapeDtypeStruct((m, LANES), F32),
        ],
        compiler_params=_cparams(1),
        name="mla_prep",
    )(mlat, gq.reshape(1, q_lora), gkv.reshape(1, kv_lora), bf_tile, cos_t, sin_t)


def _q_up_kernel(a_ref, w_ref, cos_ref, sin_ref, o_ref, *, n_heads, out_scale):
    acc = _dot(a_ref[...], w_ref[...]) * out_scale
    cos = cos_ref[...]
    sin = sin_ref[...]
    for h in range(n_heads):
        base = h * MLA_QK_PAD
        o_ref[:, base:base + MLA_NOPE] = acc[:, base:base + MLA_NOPE].astype(BF16)
        t = acc[:, base + MLA_NOPE:base + MLA_QK_PAD]
        o_ref[:, base + MLA_NOPE:base + MLA_QK_PAD] = _rope_tile(t, cos, sin).astype(BF16)


def _q_up(qlat, wq, cos_t, sin_t, n_heads, out_scale):
    m, k = qlat.shape
    n = wq.shape[1]
    s = cos_t.shape[0]
    tr = _tile(s, 512, SUBLANES)
    ns = s // tr
    table = pl.BlockSpec((tr, LANES), lambda i: (i % ns, 0))
    return pl.pallas_call(
        functools.partial(_q_up_kernel, n_heads=n_heads, out_scale=out_scale),
        grid=(m // tr,),
        in_specs=[pl.BlockSpec((tr, k), lambda i: (i, 0)), pl.BlockSpec((k, n), lambda i: (0, 0)), table, table],
        out_specs=pl.BlockSpec((tr, n), lambda i: (i, 0)),
        out_shape=jax.ShapeDtypeStruct((m, n), BF16),
        compiler_params=_cparams(1),
        name="mla_q_up",
    )(qlat, wq, cos_t, sin_t)


def _kv_up_kernel(a_ref, wk_ref, wv_ref, kr_ref, k_ref, v_ref, *, n_heads):
    a = a_ref[...].astype(BF16)
    kn = _dot(a, wk_ref[...])
    v_ref[...] = _dot(a, wv_ref[...]).astype(BF16)
    kr = kr_ref[...]
    lane = lax.broadcasted_iota(jnp.int32, kr.shape, 1)
    krm = jnp.where(lane < LANES // 2, kr, 0.0).astype(BF16)
    for h in range(n_heads):
        base = h * MLA_QK_PAD
        k_ref[:, base:base + MLA_NOPE] = kn[:, h * MLA_NOPE:(h + 1) * MLA_NOPE].astype(BF16)
        k_ref[:, base + MLA_NOPE:base + MLA_QK_PAD] = krm


def _kv_up(ckv, wk, wv, kr, n_heads):
    m, k = ckv.shape
    tm = _tile(m, 512, SUBLANES)
    rows = lambda w: pl.BlockSpec((tm, w), lambda i: (i, 0))
    whole = lambda a: pl.BlockSpec(a.shape, lambda i: (0, 0))
    return pl.pallas_call(
        functools.partial(_kv_up_kernel, n_heads=n_heads),
        grid=(m // tm,),
        in_specs=[rows(k), whole(wk), whole(wv), rows(LANES)],
        out_specs=[rows(n_heads * MLA_QK_PAD), rows(n_heads * MLA_V)],
        out_shape=[
            jax.ShapeDtypeStruct((m, n_heads * MLA_QK_PAD), BF16),
            jax.ShapeDtypeStruct((m, n_heads * MLA_V), BF16),
        ],
        compiler_params=_cparams(1),
        name="mla_kv_up",
    )(ckv, wk, wv, kr)


def _split3(x):
    x1 = x.astype(BF16)
    r1 = x - x1.astype(F32)
    x2 = r1.astype(BF16)
    x3 = (r1 - x2.astype(F32)).astype(BF16)
    return x1, x2, x3


def _cumsum_kernel(x_ref, o_ref, *rest, tc, bias_lanes):
    carry_ref = rest[-1]

    @pl.when(pl.program_id(1) == 0)
    def _():
        carry_ref[...] = jnp.zeros_like(carry_ref)

    r = lax.broadcasted_iota(jnp.int32, (tc, tc), 0)
    c = lax.broadcasted_iota(jnp.int32, (tc, tc), 1)
    tri = jnp.where(c <= r, 1.0, 0.0).astype(BF16)
    x1, x2, x3 = _split3(x_ref[0])
    y = (_dot(tri, x3) + _dot(tri, x2)) + _dot(tri, x1) + carry_ref[...]
    o_ref[0] = y
    carry_ref[...] = y[tc - 1:tc, :]
    if bias_lanes is not None:
        kb_ref = rest[0]
        lane0, n_heads = bias_lanes
        parts = [p.astype(F32) for p in _split3(y * (-LOG2E))]
        lane = lax.broadcasted_iota(jnp.int32, (tc, LANES), 1)
        for h in range(n_heads):
            col = [jnp.broadcast_to(p[:, lane0 + h:lane0 + h + 1], (tc, LANES)) for p in parts]
            blk = jnp.where(lane == 0, col[0], jnp.where(lane == 1, col[1], jnp.where(lane == 2, col[2], 0.0)))
            kb_ref[0, :, h * LANES:(h + 1) * LANES] = blk.astype(BF16)


def _cumsum_rows(x, bias_lanes=None):
    b, s, w = x.shape
    tc = _tile(s, 256, 2 * SUBLANES)
    blk = pl.BlockSpec((1, tc, w), lambda bi, j: (bi, j, 0))
    out_specs, out_shape = [blk], [jax.ShapeDtypeStruct((b, s, w), F32)]
    if bias_lanes is not None:
        wide = bias_lanes[1] * LANES
        out_specs.append(pl.BlockSpec((1, tc, wide), lambda bi, j: (bi, j, 0)))
        out_shape.append(jax.ShapeDtypeStruct((b, s, wide), BF16))
    outs = pl.pallas_call(
        functools.partial(_cumsum_kernel, tc=tc, bias_lanes=bias_lanes),
        grid=(b, s // tc),
        in_specs=[blk],
        out_specs=out_specs,
        out_shape=out_shape,
        scratch_shapes=[pltpu.VMEM((1, w), F32)],
        compiler_params=_cparams(2, sequential_axes=(1,)),
        name="cumsum_rows",
    )(x)
    return outs if bias_lanes is not None else outs[0]


def _rep_lanes(x, width):
    return x if width == LANES else jnp.concatenate([x] * (width // LANES), axis=1)


def _softmax_attn_kernel(*refs, t, group, dq, dv, chunk, has_bias):
    if has_bias:
        qt_ref, k_ref, kb_ref, vt_ref, o_ref, m_ref, l_ref, acc_ref, sa_ref, sb_ref = refs
    else:
        qt_ref, k_ref, vt_ref, o_ref, m_ref, l_ref, acc_ref, sa_ref, sb_ref = refs
    i = pl.program_id(2)
    m_ref[...] = jnp.full_like(m_ref, -jnp.inf)
    l_ref[...] = jnp.zeros_like(l_ref)
    acc_ref[...] = jnp.zeros_like(acc_ref)
    if has_bias:
        row = lax.broadcasted_iota(jnp.int32, (LANES, t), 0)
        ones_rows = jnp.where(row < N_BIAS_TERMS, 1.0, 0.0).astype(BF16)

    def score(g, ks):
        k = k_ref[0, pl.ds(ks, t), g * dq:(g + 1) * dq]
        qt = qt_ref[0, g * dq:(g + 1) * dq, :]
        if has_bias:
            k = jnp.concatenate([k, kb_ref[0, pl.ds(ks, t), g * LANES:(g + 1) * LANES]], axis=1)
            qt = jnp.concatenate([qt, ones_rows], axis=0)
        return _dot(k, qt)

    heads = range(group)

    def scores_into(j, s_ref):
        ks = pl.multiple_of(j * t, t)
        for g in heads:
            s_ref[g] = score(g, ks)

    def fold(j, s_ref, masked):
        ks = pl.multiple_of(j * t, t)
        probs, alphas = [], []
        for g in heads:
            s = s_ref[g]
            if masked:
                kpos = lax.broadcasted_iota(jnp.int32, s.shape, 0)
                qpos = lax.broadcasted_iota(jnp.int32, s.shape, 1)
                visible = (kpos // chunk) <= (qpos // chunk) if chunk else kpos <= qpos
                s = jnp.where(visible, s, -jnp.inf)
            m_prev = m_ref[g]
            m_new = jnp.maximum(m_prev, jnp.max(s, axis=0, keepdims=True))
            alpha = jnp.exp2(m_prev - m_new)
            p = jnp.exp2(s - m_new)
            l_ref[g] = alpha * l_ref[g] + jnp.sum(p, axis=0, keepdims=True)
            m_ref[g] = m_new
            probs.append(p.astype(BF16))
            alphas.append(alpha)
        for g in heads:
            pv = _dot(vt_ref[0, g * dv:(g + 1) * dv, pl.ds(ks, t)], probs[g])
            acc_ref[g] = alphas[g] * acc_ref[g] + pv

    scores_into(0, sa_ref)

    def body(jj, carry):
        j = 2 * jj
        scores_into(j + 1, sb_ref)
        fold(j, sa_ref, False)
        scores_into(j + 2, sa_ref)
        fold(j + 1, sb_ref, False)
        return carry

    lax.fori_loop(0, i // 2, body, 0)

    @pl.when(i % 2 == 0)
    def _():
        fold(i, sa_ref, True)

    @pl.when(i % 2 == 1)
    def _():
        scores_into(i, sb_ref)
        fold(i - 1, sa_ref, False)
        fold(i, sb_ref, True)

    for g in range(group):
        o_ref[0, :, g * dv:(g + 1) * dv] = (acc_ref[g] / l_ref[g]).T


def _head_group(n_heads):
    return ATTN_HEAD_GROUP if n_heads % ATTN_HEAD_GROUP == 0 else 1


def _softmax_attn(qt, k, vt, n_heads, dq, dv, chunk, name, kb=None):
    b, s, _ = k.shape
    t = _tile(s, ATTN_TILE, LANES)
    grp = _head_group(n_heads)
    stat = pltpu.VMEM((grp, 1, t), F32)
    keys = lambda w: pl.BlockSpec((1, s, grp * w), lambda bi, h, i: (bi, 0, h))
    in_specs = [pl.BlockSpec((1, grp * dq, t), lambda bi, h, i: (bi, h, i)), keys(dq)]
    args = [qt, k]
    if kb is not None:
        in_specs.append(keys(LANES))
        args.append(kb)
    in_specs.append(pl.BlockSpec((1, grp * dv, s), lambda bi, h, i: (bi, h, 0)))
    args.append(vt)
    return pl.pallas_call(
        functools.partial(_softmax_attn_kernel, t=t, group=grp, dq=dq, dv=dv, chunk=chunk, has_bias=kb is not None),
        grid=(b, n_heads // grp, s // t),
        in_specs=in_specs,
        out_specs=pl.BlockSpec((1, t, grp * dv), lambda bi, h, i: (bi, i, h)),
        out_shape=jax.ShapeDtypeStruct((b, s, n_heads * dv), F32),
        scratch_shapes=[stat, stat, pltpu.VMEM((grp, dv, t), F32),
                        pltpu.VMEM((grp, t, t), F32), pltpu.VMEM((grp, t, t), F32)],
        compiler_params=_cparams(3),
        name=name,
    )(*args)


def _softmax_attn_cached_kernel(*refs, has_bias, chunk, past):
    if has_bias:
        q_ref, kc_ref, vc_ref, kn_ref, vn_ref, fkc_ref, fkn_ref, o_ref = refs
    else:
        q_ref, kc_ref, vc_ref, kn_ref, vn_ref, o_ref = refs
    q = q_ref[...]
    s_c = _dot_nt(q, kc_ref[...].astype(BF16))
    s_n = _dot_nt(q, kn_ref[...])
    if has_bias:
        s_c = s_c - fkc_ref[...] * LOG2E
        s_n = s_n - fkn_ref[...] * LOG2E
    r = lax.broadcasted_iota(jnp.int32, s_n.shape, 0) + past
    c = lax.broadcasted_iota(jnp.int32, s_n.shape, 1) + past
    visible = (c // chunk) <= (r // chunk) if chunk else c <= r
    s_n = jnp.where(visible, s_n, -jnp.inf)
    m = jnp.maximum(jnp.max(s_c, axis=-1, keepdims=True), jnp.max(s_n, axis=-1, keepdims=True))
    p_c = jnp.exp2(s_c - m)
    p_n = jnp.exp2(s_n - m)
    l = jnp.sum(p_c, axis=-1, keepdims=True) + jnp.sum(p_n, axis=-1, keepdims=True)
    acc = _dot(p_c.astype(BF16), vc_ref[...].astype(BF16)) + _dot(p_n.astype(BF16), vn_ref[...])
    o_ref[...] = acc / l


def _cache_spec(cache, layer, d):
    if layer is None:
        return pl.BlockSpec((None, cache.shape[1], d), lambda bi, h: (bi, 0, h))
    return pl.BlockSpec((None, None, cache.shape[2], d), lambda bi, h: (layer, bi, 0, h))


def _softmax_attn_cached(q, kc, vc, kn, vn, n_heads, dq, dv, chunk, f_row_c=None, f_row_n=None, layer=None):
    b, sn, _ = q.shape
    past = kc.shape[-2]
    has_bias = f_row_c is not None
    new = lambda d: pl.BlockSpec((None, sn, d), lambda bi, h: (bi, 0, h))
    in_specs = [new(dq), _cache_spec(kc, layer, dq), _cache_spec(vc, layer, dv), new(dq), new(dv)]
    args = [q, kc, vc, kn, vn]
    if has_bias:
        in_specs += [
            pl.BlockSpec((None, None, 1, past), lambda bi, h: (bi, h, 0, 0)),
            pl.BlockSpec((None, None, 1, sn), lambda bi, h: (bi, h, 0, 0)),
        ]
        args += [f_row_c, f_row_n]
    return pl.pallas_call(
        functools.partial(_softmax_attn_cached_kernel, has_bias=has_bias, chunk=chunk, past=past),
        grid=(b, n_heads),
        in_specs=in_specs,
        out_specs=new(dv),
        out_shape=jax.ShapeDtypeStruct((b, sn, n_heads * dv), F32),
        compiler_params=_cparams(2),
        name="fox_attn_cached" if has_bias else "mla_attn_cached",
    )(*args)


def _suffix_matrix(t):
    r = lax.broadcasted_iota(jnp.int32, (t, t), 0)
    c = lax.broadcasted_iota(jnp.int32, (t, t), 1)
    return jnp.where(r >= c, 1.0, 0.0).astype(BF16)


def _sb_fold(q, k, v, u, c, strict):
    z = _dot_nt(q, k)
    neg_abs = lax.bitcast_convert_type(lax.bitcast_convert_type(z, jnp.uint32) | jnp.uint32(0x80000000), F32)
    sp = jnp.maximum(z, 0.0) + jnp.log(1.0 + jnp.exp2(neg_abs)) * LOG2E
    if strict is not None:
        sp = jnp.where(strict, sp, 0.0)
    hi = lax.bitcast_convert_type(lax.bitcast_convert_type(sp, jnp.uint32) & jnp.uint32(0xFFFF0000), F32)
    lo = (sp - hi).astype(BF16)
    suffix = _dot(lo, u) + _dot(hi.astype(BF16), u)
    width = z.shape[1]
    c_wide = c[:, :width] if width < LANES else _rep_lanes(c, width)
    a = jnp.exp2(z - (suffix + c_wide))
    if strict is not None:
        a = jnp.where(strict, a, 0.0)
    return _dot(a.astype(BF16), v), c + jnp.sum(sp, axis=-1, keepdims=True)


def _sb_attn_kernel(qt_ref, k_ref, vt_ref, o_ref, c_ref, acc_ref, *bufs, t, tsub, group, d):
    i = pl.program_id(2)
    buf_a, buf_b = bufs[:3], bufs[3:]
    r = lax.broadcasted_iota(jnp.int32, (tsub, tsub), 0)
    col = lax.broadcasted_iota(jnp.int32, (tsub, tsub), 1)
    ut = jnp.where(col >= r, 1.0, 0.0).astype(BF16)
    c_ref[...] = jnp.zeros_like(c_ref)
    acc_ref[...] = jnp.zeros_like(acc_ref)
    n_sub = t // tsub
    chains = [(g, sub) for g in range(group) for sub in reversed(range(n_sub))]

    def strict_mask(sub):
        kpos = lax.broadcasted_iota(jnp.int32, (tsub, t), 0) + sub * tsub
        qpos = lax.broadcasted_iota(jnp.int32, (tsub, t), 1)
        return kpos < qpos

    def scores(j, buf, masked):
        z_ref, hi_ref, lo_ref = buf
        z = {}
        for g, sub in chains:
            start = pl.multiple_of(j * t + sub * tsub, tsub)
            z[g, sub] = _dot(k_ref[0, pl.ds(start, tsub), g * d:(g + 1) * d], qt_ref[0, g * d:(g + 1) * d, :])
        for g, sub in chains:
            zc = z[g, sub]
            neg_abs = lax.bitcast_convert_type(lax.bitcast_convert_type(zc, jnp.uint32) | jnp.uint32(0x80000000), F32)
            sp = jnp.maximum(zc, 0.0) + jnp.log(1.0 + jnp.exp2(neg_abs)) * LOG2E
            if masked:
                sp = jnp.where(strict_mask(sub), sp, 0.0)
            hi = lax.bitcast_convert_type(lax.bitcast_convert_type(sp, jnp.uint32) & jnp.uint32(0xFFFF0000), F32)
            z_ref[g, sub] = zc
            hi_ref[g, sub] = hi.astype(BF16)
            lo_ref[g, sub] = (sp - hi).astype(BF16)

    def fold(j, buf, masked):
        z_ref, hi_ref, lo_ref = buf
        suffix = {(g, sub): _dot(ut, lo_ref[g, sub]) + _dot(ut, hi_ref[g, sub]) for g, sub in chains}
        c = {g: c_ref[g] for g in range(group)}
        acc = {g: acc_ref[g] for g in range(group)}
        for g, sub in chains:
            start = pl.multiple_of(j * t + sub * tsub, tsub)
            a = jnp.exp2(z_ref[g, sub] - (suffix[g, sub] + c[g]))
            if masked:
                a = jnp.where(strict_mask(sub), a, 0.0)
            acc[g] = acc[g] + _dot(vt_ref[0, g * d:(g + 1) * d, pl.ds(start, tsub)], a.astype(BF16))
            c[g] = c[g] + suffix[g, sub][0:1, :]
        for g in range(group):
            c_ref[g] = c[g]
            acc_ref[g] = acc[g]

    scores(i, buf_a, True)
    fold(i, buf_a, True)
    block = lambda n: jnp.maximum(i - 1 - n, 0)
    scores(block(0), buf_a, False)

    def body(jj, carry):
        n = 2 * jj
        scores(block(n + 1), buf_b, False)
        fold(block(n), buf_a, False)
        scores(block(n + 2), buf_a, False)
        fold(block(n + 1), buf_b, False)
        return carry

    lax.fori_loop(0, i // 2, body, 0)

    @pl.when(i % 2 == 1)
    def _():
        fold(0, buf_a, False)

    for g in range(group):
        o_ref[0, :, g * d:(g + 1) * d] = acc_ref[g].T


def _sb_attn(qt, k, vt, n_heads):
    b, s, _ = k.shape
    d = HEAD_DIM
    t = _tile(s, ATTN_TILE, LANES)
    tsub = _tile(t, SB_SUB_TILE, LANES)
    grp = _head_group(n_heads)
    return pl.pallas_call(
        functools.partial(_sb_attn_kernel, t=t, tsub=tsub, group=grp, d=d),
        grid=(b, n_heads // grp, s // t),
        in_specs=[
            pl.BlockSpec((1, grp * d, t), lambda bi, h, i: (bi, h, i)),
            pl.BlockSpec((1, s, grp * d), lambda bi, h, i: (bi, 0, h)),
            pl.BlockSpec((1, grp * d, s), lambda bi, h, i: (bi, h, 0)),
        ],
        out_specs=pl.BlockSpec((1, t, grp * d), lambda bi, h, i: (bi, i, h)),
        out_shape=jax.ShapeDtypeStruct((b, s, n_heads * d), F32),
        scratch_shapes=[pltpu.VMEM((grp, 1, t), F32), pltpu.VMEM((grp, d, t), F32)]
        + [pltpu.VMEM((grp, t // tsub, tsub, t), dt) for dt in (F32, BF16, BF16)] * 2,
        compiler_params=_cparams(3),
        name="sb_attn",
    )(qt, k, vt)


def _sb_attn_cached_kernel(q_ref, kc_ref, vc_ref, kn_ref, vn_ref, o_ref, *, tk):
    q = q_ref[...]
    sn = q.shape[0]
    r = lax.broadcasted_iota(jnp.int32, (sn, sn), 0)
    col = lax.broadcasted_iota(jnp.int32, (sn, sn), 1)
    acc, c = _sb_fold(q, kn_ref[...], vn_ref[...], _suffix_matrix(sn), jnp.zeros((sn, LANES), F32), col < r)
    u = _suffix_matrix(tk)
    for blk in reversed(range(kc_ref.shape[0] // tk)):
        k = kc_ref[blk * tk:(blk + 1) * tk, :].astype(BF16)
        v = vc_ref[blk * tk:(blk + 1) * tk, :].astype(BF16)
        pv, c = _sb_fold(q, k, v, u, c, None)
        acc = acc + pv
    o_ref[...] = acc


def _sb_attn_cached(q, kc, vc, kn, vn, n_heads, layer=None):
    b, sn, _ = q.shape
    d = HEAD_DIM
    tk = _tile(kc.shape[-2], SB_SUB_TILE, LANES)
    new = pl.BlockSpec((None, sn, d), lambda bi, h: (bi, 0, h))
    old = _cache_spec(kc, layer, d)
    return pl.pallas_call(
        functools.partial(_sb_attn_cached_kernel, tk=tk),
        grid=(b, n_heads),
        in_specs=[new, old, old, new, new],
        out_specs=new,
        out_shape=jax.ShapeDtypeStruct((b, sn, n_heads * d), F32),
        compiler_params=_cparams(2),
        name="sb_attn_cached",
    )(q, kc, vc, kn, vn)


def _group_norm_kernel(of_ref, os_ref, om_ref, g_ref, h_ref):
    off = 0
    for ref in (of_ref, os_ref, om_ref):
        w = ref.shape[-1]
        h_ref[0, :, off:off + w] = (_rms(ref[0]) * g_ref[:, off:off + w]).astype(BF16)
        off += w


def _group_norm(o_fox, o_sb, o_mla, g):
    b, s, _ = o_fox.shape
    d = g.shape[-1]
    tr = _tile(s, 256, SUBLANES)
    rows = lambda a: pl.BlockSpec((1, tr, a.shape[-1]), lambda bi, i: (bi, i, 0))
    return pl.pallas_call(
        _group_norm_kernel,
        grid=(b, s // tr),
        in_specs=[rows(o_fox), rows(o_sb), rows(o_mla), pl.BlockSpec((1, d), lambda bi, i: (0, 0))],
        out_specs=pl.BlockSpec((1, tr, d), lambda bi, i: (bi, i, 0)),
        out_shape=jax.ShapeDtypeStruct((b, s, d), BF16),
        compiler_params=_cparams(2),
        name="group_norm",
    )(o_fox, o_sb, o_mla, g.reshape(1, d))


def _rope_tables(pos, rope_dim):
    half = rope_dim // 2
    inv = ROPE_THETA ** (-jnp.arange(half, dtype=F32) / half)
    ang = pos.astype(F32)[:, None] * inv[None, :]
    cos, sin = jnp.cos(ang), jnp.sin(ang)
    n = pos.shape[0]
    pad = LANES - rope_dim
    cos_t = jnp.concatenate([cos, cos, jnp.ones((n, pad), F32)], axis=-1)
    sin_t = jnp.concatenate([-sin, sin, jnp.zeros((n, pad), F32)], axis=-1)
    return cos_t, sin_t


def _prep_layer_weights(l, dims, w_in, b_forget, w_q_up, w_kv_up):
    hf, hs, hm, q_lora, kv_lora, rope = dims
    wf, ws = hf * HEAD_DIM, hs * HEAD_DIM
    d = w_in.shape[1]
    w = w_in[l]
    widths = (wf, wf, wf, hf, ws, ws, ws, q_lora, kv_lora, rope)
    offs = [0]
    for x in widths:
        offs.append(offs[-1] + x)
    col = lambda i: w[:, offs[i]:offs[i + 1]]
    lw = {"fox_cols": (offs[0], offs[1], offs[2])}
    for name, i in (("sq", 4), ("sk", 5), ("sv", 6)):
        lw[name] = col(i).astype(BF16)
    pad = jnp.zeros((d, LANES - rope - hf), F32)
    lw["mla_in"] = jnp.concatenate([col(7), col(8), col(9), col(3), pad], axis=1).astype(BF16)
    lw["bf"] = jnp.zeros((1, LANES), F32).at[0, rope:rope + hf].set(b_forget[l])
    wq = w_q_up[l].reshape(q_lora, hm, MLA_NOPE + rope)
    wq = jnp.pad(wq, ((0, 0), (0, 0), (0, MLA_QK_PAD - MLA_NOPE - rope)))
    lw["wq"] = wq.reshape(q_lora, hm * MLA_QK_PAD).astype(BF16)
    wkv = w_kv_up[l].reshape(kv_lora, hm, MLA_NOPE + MLA_V)
    lw["wk"] = wkv[:, :, :MLA_NOPE].reshape(kv_lora, hm * MLA_NOPE).astype(BF16)
    lw["wv"] = wkv[:, :, MLA_NOPE:].reshape(kv_lora, hm * MLA_V).astype(BF16)
    return lw


def _ffn(h, gw, l, f):
    b, s, d = h.shape
    a = _swiglu(h.reshape(b * s, d), gw["wg"], gw["wu"], (l, f))
    (y,) = _mm(a, gw["wd"], (F32,), "ffn_down", w_index=(l, f), tm_pref=512, tn_pref=512)
    return y.reshape(b, s, d)


def _token_mix(h, l, lw, gw, dims, tables, cache, stacked):
    hf, hs, hm, q_lora, kv_lora, rope = dims
    b, s, d = h.shape
    h2 = h.reshape(b * s, d)
    cos_t, sin_t = tables
    three = lambda a: a.reshape(b, s, -1)
    depth = gw["w_in"].shape[0]

    qk_scale = HEAD_DIM ** -0.5 * LOG2E
    mla_scale = (MLA_NOPE + rope) ** -0.5 * LOG2E
    fq_col, fk_col, fv_col = lw["fox_cols"]
    wf = hf * HEAD_DIM
    stacked = dict(stacked)
    (fq,) = _mm(h2, gw["w_in"], (BF16,), "in_fox_q", w_index=(l,), col0=fq_col, n=wf, out_scale=qk_scale)
    stacked["fk"], fk16 = _mm_state(h2, gw["w_in"], (l,), fk_col, hf, l, stacked["fk"], depth, "in_fox_k")
    stacked["fv"], fv16 = _mm_state(h2, gw["w_in"], (l,), fv_col, hf, l, stacked["fv"], depth, "in_fox_v")
    (sq,) = _mm(h2, lw["sq"], (BF16,), "in_sb_q", out_scale=qk_scale)
    stacked["sk"], sk16 = _mm_state(h2, lw["sk"], (), 0, hs, l, stacked["sk"], depth, "in_sb_k")
    stacked["sv"], sv16 = _mm_state(h2, lw["sv"], (), 0, hs, l, stacked["sv"], depth, "in_sb_v")
    (mlat,) = _mm(h2, lw["mla_in"], (F32,), "in_mla", tm_pref=512, tn_pref=lw["mla_in"].shape[1])

    qlat, ckv32, ckv16, krlf = _mla_prep(mlat, lw["gq"], lw["gkv"], lw["bf"], cos_t, sin_t, q_lora, kv_lora, hf)
    q_mla = _q_up(qlat, lw["wq"], cos_t, sin_t, hm, mla_scale)
    k_mla, v_mla = _kv_up(ckv16, lw["wk"], lw["wv"], krlf, hm)

    krlf3 = three(krlf)
    krope = krlf3[:, :, :rope]
    logf = krlf3[:, :, rope:rope + hf]

    if cache is None:
        transposed = lambda a: jnp.swapaxes(three(a), 1, 2)
        _, k_bias = _cumsum_rows(krlf3, bias_lanes=(rope, hf))
        o_fox = _softmax_attn(transposed(fq), three(fk16), transposed(fv16), hf, HEAD_DIM, HEAD_DIM, 0,
                              "fox_attn", kb=k_bias)
        o_sb = _sb_attn(transposed(sq), three(sk16), transposed(sv16), hs)
        o_mla = _softmax_attn(transposed(q_mla), three(k_mla), transposed(v_mla), hm, MLA_QK_PAD, MLA_V, CHUNK, "mla_attn")
    else:
        c_fk, c_fv, c_logf, c_sk, c_sv, c_ckv, c_krope = cache
        past = c_fk.shape[2]
        flat = lambda a: a.reshape(b, past, -1)
        logf_all = jnp.concatenate([c_logf[l].astype(F32), logf], axis=1)
        logf_all = jnp.pad(logf_all, ((0, 0), (0, 0), (0, LANES - hf)))
        f_all = _cumsum_rows(logf_all)[:, :, :hf]
        f_bh = jnp.swapaxes(f_all, 1, 2)
        o_fox = _softmax_attn_cached(
            three(fq), c_fk, c_fv, three(fk16), three(fv16), hf, HEAD_DIM, HEAD_DIM, 0,
            f_row_c=f_bh[:, :, None, :past], f_row_n=f_bh[:, :, None, past:], layer=l)
        o_sb = _sb_attn_cached(three(sq), c_sk, c_sv, three(sk16), three(sv16), hs, layer=l)
        c_kr = jnp.pad(c_krope[l].reshape(b * past, rope), ((0, 0), (0, LANES - rope)))
        kc_mla, vc_mla = _kv_up(c_ckv[l].reshape(b * past, kv_lora), lw["wk"], lw["wv"], c_kr, hm)
        o_mla = _softmax_attn_cached(
            three(q_mla), flat(kc_mla), flat(vc_mla), three(k_mla), three(v_mla), hm, MLA_QK_PAD, MLA_V, CHUNK)

    hmix = _group_norm(o_fox, o_sb, o_mla, lw["g_mix"])
    (mix,) = _mm(hmix.reshape(b * s, d), gw["w_out"], (F32,), "mix_out", w_index=(l,))
    return mix.reshape(b, s, d), (logf, ckv32.reshape(b, s, kv_lora), krope), stacked


def kernel(x_prompt, x_sample, c_prompt, c_sample, cache_fox_k, cache_fox_v, cache_fox_logf, cache_sb_k, cache_sb_v, cache_mla_ckv, cache_mla_krope, w_ada, b_ada, g_norm, w_in, b_forget, g_q_lat, w_q_up, g_kv_lat, w_kv_up, g_mix_out, w_out, w_ffn_gate, w_ffn_up, w_ffn_down, g_final):
    depth = w_ada.shape[0]
    d = x_prompt.shape[-1]
    nbp, nbs = x_prompt.shape[0], x_sample.shape[0]
    past = cache_fox_k.shape[2]
    hf, hs = cache_fox_k.shape[3], cache_sb_k.shape[3]
    kv_lora, rope, q_lora = cache_mla_ckv.shape[-1], cache_mla_krope.shape[-1], g_q_lat.shape[-1]
    hm = w_q_up.shape[-1] // (MLA_NOPE + rope)
    dims = (hf, hs, hm, q_lora, kv_lora, rope)

    c_all = jnp.concatenate([c_prompt, c_sample], axis=0)
    nb = nbp + nbs
    nb_pad = -(-nb // SUBLANES) * SUBLANES
    c_all = jnp.pad(c_all, ((0, nb_pad - nb), (0, 0)))
    mods = _ada(c_all, w_ada, b_ada).reshape(depth, nb_pad, N_MOD, 1, d)
    groups = {"p": (0, nbp), "s": (nbp, nb)}
    mod = lambda l, grp, i: mods[l, groups[grp][0]:groups[grp][1], i]

    tables = {
        "p": _rope_tables(jnp.arange(x_prompt.shape[1]), rope),
        "s": _rope_tables(past + jnp.arange(x_sample.shape[1]), rope),
    }
    lanes = lambda a: a.reshape(a.shape[:3] + (-1,))
    caches = (lanes(cache_fox_k), lanes(cache_fox_v), cache_fox_logf, lanes(cache_sb_k), lanes(cache_sb_v),
              cache_mla_ckv, cache_mla_krope)

    gw = {"w_in": w_in.astype(BF16), "w_out": w_out.astype(BF16), "wg": w_ffn_gate.astype(BF16),
          "wu": w_ffn_up.astype(BF16), "wd": w_ffn_down.astype(BF16)}

    xs = {"p": x_prompt, "s": x_sample}
    pending = {"p": None, "s": None}
    small = {"p": [], "s": []}
    stacked = {grp: {"fk": None, "fv": None, "sk": None, "sv": None} for grp in ("p", "s")}
    for l in range(depth):
        lw = _prep_layer_weights(l, dims, w_in, b_forget, w_q_up, w_kv_up)
        lw["gq"], lw["gkv"], lw["g_mix"] = g_q_lat[l], g_kv_lat[l], g_mix_out[l]
        for grp in ("p", "s"):
            m = lambda i: mod(l, grp, i)
            x, h = _prenorm(xs[grp], g_norm[l, 0], m(1), m(0), res=pending[grp])
            y = _ffn(h, gw, l, 0)
            x, h = _prenorm(x, g_norm[l, 1], m(4), m(3), res=(y, m(2), 0.5))
            cache = None if grp == "p" else caches
            mix, st, stacked[grp] = _token_mix(h, l, lw, gw, dims, tables[grp], cache, stacked[grp])
            small[grp].append(st)
            x, h = _prenorm(x, g_norm[l, 2], m(7), m(6), res=(mix, m(5), 1.0))
            y = _ffn(h, gw, l, 1)
            xs[grp] = x
            pending[grp] = (y, m(8), 0.5)

    y_prompt = _prenorm(xs["p"], g_final, res=pending["p"], final=True)
    y_sample = _prenorm(xs["s"], g_final, res=pending["s"], final=True)

    def states(grp, x):
        bsz, seq = x.shape[0], x.shape[1]
        big = lambda name: stacked[grp][name].reshape((depth, bsz, seq) + stacked[grp][name].shape[2:])
        stack = lambda i: jnp.stack([st[i] for st in small[grp]])
        return (big("fk"), big("fv"), stack(0), big("sk"), big("sv"), stack(1), stack(2))

    return (y_prompt, y_sample) + states("p", x_prompt) + states("s", x_sample)
```

```python
import functools

import jax
import jax.numpy as jnp
from jax import lax
from jax.experimental import pallas as pl
from jax.experimental.pallas import tpu as pltpu

F32 = jnp.float32
BF16 = jnp.bfloat16

HEAD_DIM = 128
MLA_NOPE = 128
MLA_V = 128
CHUNK = 64
N_MOD = 9
EPS = 1e-6
ROPE_THETA = 10000.0

LANES = 128
SUBLANES = 8
MLA_QK_PAD = 256
VMEM_LIMIT = 56 * 1024 * 1024
LOG2E = 1.4426950408889634

ATTN_TILE = 512
ATTN_HEAD_GROUP = 2
N_BIAS_TERMS = 3
CACHE_BLOCK = 1024
SB_SUB_TILE = 256


def _cparams(n_axes, sequential_axes=()):
    sem = tuple("arbitrary" if a in sequential_axes else "parallel" for a in range(n_axes))
    return pltpu.CompilerParams(dimension_semantics=sem, vmem_limit_bytes=VMEM_LIMIT)


def _tile(n, pref, align):
    if n <= pref:
        return n
    for t in range(pref - pref % align, 0, -align):
        if n % t == 0:
            return t
    return n


def _dot(a, b):
    return jnp.dot(a, b, preferred_element_type=F32)


def _dot_nt(a, b):
    return lax.dot_general(a, b, (((1,), (1,)), ((), ())), preferred_element_type=F32)


def _softplus_neg_abs(z):
    return jnp.log1p(jnp.exp(-jnp.abs(z)))


def _rms(x):
    return x * lax.rsqrt(jnp.mean(x * x, axis=-1, keepdims=True) + EPS)


def _rope_tile(t, cos, sin):
    lane = lax.broadcasted_iota(jnp.int32, t.shape, 1)
    half = LANES // 4
    partner = jnp.where(lane < half, pltpu.roll(t, LANES - half, 1), pltpu.roll(t, half, 1))
    return t * cos + partner * sin


def _ada_kernel(c_ref, w_ref, b_ref, o_ref):
    c = c_ref[...]
    a = (c / (1.0 + jnp.exp(-c))).astype(BF16)
    o_ref[0] = _dot(a, w_ref[0].astype(BF16)) + b_ref[0]


def _ada(c_all, w_ada, b_ada):
    depth, d, n = w_ada.shape
    nb = c_all.shape[0]
    tn = _tile(n, 512, LANES)
    return pl.pallas_call(
        _ada_kernel,
        grid=(depth, n // tn),
        in_specs=[
            pl.BlockSpec((nb, d), lambda l, j: (0, 0)),
            pl.BlockSpec((1, d, tn), lambda l, j: (l, 0, j)),
            pl.BlockSpec((1, 1, tn), lambda l, j: (l, 0, j)),
        ],
        out_specs=pl.BlockSpec((1, nb, tn), lambda l, j: (l, 0, j)),
        out_shape=jax.ShapeDtypeStruct((depth, nb, n), F32),
        compiler_params=_cparams(2),
        name="ada",
    )(c_all, w_ada, b_ada.reshape(depth, 1, n))


def _prenorm_kernel(*refs, coef, has_res, final):
    refs = list(refs)
    x_ref = refs.pop(0)
    x = x_ref[0]
    if has_res:
        y_ref = refs.pop(0)
        gate_ref = refs.pop(0)
        x = x + (coef * gate_ref[0]) * y_ref[0]
    g_ref = refs.pop(0)
    if final:
        (o_ref,) = refs
        o_ref[0] = _rms(x) * g_ref[...]
        return
    sc_ref = refs.pop(0)
    sh_ref = refs.pop(0)
    if has_res:
        xo_ref = refs.pop(0)
        xo_ref[0] = x
    (h_ref,) = refs
    h = (_rms(x) * g_ref[...]) * (1.0 + sc_ref[0]) + sh_ref[0]
    h_ref[0] = h.astype(BF16)


def _prenorm(x, g, scale=None, shift=None, res=None, final=False):
    b, s, d = x.shape
    tr = _tile(s, 256, SUBLANES)
    row = pl.BlockSpec((1, tr, d), lambda bi, i: (bi, i, 0))
    per_b = pl.BlockSpec((1, 1, d), lambda bi, i: (bi, 0, 0))
    shared = pl.BlockSpec((1, d), lambda bi, i: (0, 0))
    args, specs = [x], [row]
    coef = 0.0
    if res is not None:
        y, gate, coef = res
        args += [y, gate]
        specs += [row, per_b]
    args.append(g.reshape(1, d))
    specs.append(shared)
    out_shape, out_specs = [], []
    if final:
        out_shape.append(jax.ShapeDtypeStruct((b, s, d), F32))
        out_specs.append(row)
    else:
        args += [scale, shift]
        specs += [per_b, per_b]
        if res is not None:
            out_shape.append(jax.ShapeDtypeStruct((b, s, d), F32))
            out_specs.append(row)
        out_shape.append(jax.ShapeDtypeStruct((b, s, d), BF16))
        out_specs.append(row)
    outs = pl.pallas_call(
        functools.partial(_prenorm_kernel, coef=coef, has_res=res is not None, final=final),
        grid=(b, s // tr),
        in_specs=specs,
        out_specs=out_specs,
        out_shape=out_shape,
        compiler_params=_cparams(2),
        name="final_norm" if final else "prenorm",
    )(*args)
    if final:
        return outs[0]
    if res is None:
        return x, outs[0]
    return outs[0], outs[1]


def _weight_spec(w, w_index, k, tn, col_block0=0):
    lead = tuple(w_index)
    return pl.BlockSpec((None,) * len(lead) + (k, tn), lambda i, j: lead + (0, col_block0 + j))


def _mm_kernel(a_ref, w_ref, *o_refs, out_scale):
    acc = _dot(a_ref[...].astype(BF16), w_ref[...])
    if out_scale != 1.0:
        acc = acc * out_scale
    for o_ref in o_refs:
        o_ref[...] = acc.astype(o_ref.dtype)


def _mm(a, w, out_dtypes, name, w_index=(), col0=0, n=None, tm_pref=1024, tn_pref=512, out_scale=1.0):
    m, k = a.shape
    n = w.shape[-1] if n is None else n
    tm = _tile(m, tm_pref, SUBLANES)
    tn = _tile(n, tn_pref, LANES)
    assert col0 % tn == 0
    outs = pl.pallas_call(
        functools.partial(_mm_kernel, out_scale=out_scale),
        grid=(m // tm, n // tn),
        in_specs=[pl.BlockSpec((tm, k), lambda i, j: (i, 0)), _weight_spec(w, w_index, k, tn, col0 // tn)],
        out_specs=[pl.BlockSpec((tm, tn), lambda i, j: (i, j)) for _ in out_dtypes],
        out_shape=[jax.ShapeDtypeStruct((m, n), dt) for dt in out_dtypes],
        compiler_params=_cparams(2),
        name=name,
    )(a, w)
    return outs


def _mm_state_kernel(a_ref, w_ref, *refs, n_heads):
    o32_ref, o16_ref = refs[-2:]
    acc = _dot(a_ref[...], w_ref[...])
    o16_ref[...] = acc.astype(BF16)
    hd = acc.shape[1] // n_heads
    for h in range(n_heads):
        o32_ref[:, h, :] = acc[:, h * hd:(h + 1) * hd]


def _mm_state(a, w, w_index, col0, n_heads, layer, stacked, depth, name, tm_pref=512):
    m, k = a.shape
    hd = HEAD_DIM
    n = n_heads * hd
    tm = _tile(m, tm_pref, 2 * SUBLANES)
    assert col0 % n == 0
    in_specs = [pl.BlockSpec((tm, k), lambda i: (i, 0)),
                pl.BlockSpec((None,) * len(w_index) + (k, n), lambda i: tuple(w_index) + (0, col0 // n))]
    args = [a, w]
    aliases = {}
    if stacked is not None:
        in_specs.append(pl.BlockSpec(memory_space=pl.ANY))
        args.append(stacked)
        aliases = {2: 0}
    return pl.pallas_call(
        functools.partial(_mm_state_kernel, n_heads=n_heads),
        grid=(m // tm,),
        in_specs=in_specs,
        out_specs=[pl.BlockSpec((None, tm, n_heads, hd), lambda i: (layer, i, 0, 0)),
                   pl.BlockSpec((tm, n), lambda i: (i, 0))],
        out_shape=[jax.ShapeDtypeStruct((depth, m, n_heads, hd), F32), jax.ShapeDtypeStruct((m, n), BF16)],
        input_output_aliases=aliases,
        compiler_params=_cparams(1),
        name=name,
    )(*args)


def _swiglu_kernel(a_ref, wg_ref, wu_ref, o_ref, *, rows):
    for r0 in range(0, a_ref.shape[0], rows):
        a = a_ref[r0:r0 + rows, :]
        g = _dot(a, wg_ref[...])
        u = _dot(a, wu_ref[...])
        o_ref[r0:r0 + rows, :] = ((g / (1.0 + jnp.exp(-g))) * u).astype(o_ref.dtype)


def _swiglu(a, wg, wu, w_index, tm_pref=2048, tn_pref=256, rows_pref=512):
    m, k = a.shape
    n = wg.shape[-1]
    tm = _tile(m, tm_pref, SUBLANES)
    tn = _tile(n, tn_pref, LANES)
    rows = _tile(tm, rows_pref, SUBLANES)
    wspec = _weight_spec(wg, w_index, k, tn)
    return pl.pallas_call(
        functools.partial(_swiglu_kernel, rows=rows),
        grid=(m // tm, n // tn),
        in_specs=[pl.BlockSpec((tm, k), lambda i, j: (i, 0)), wspec, wspec],
        out_specs=pl.BlockSpec((tm, tn), lambda i, j: (i, j)),
        out_shape=jax.ShapeDtypeStruct((m, n), BF16),
        compiler_params=_cparams(2),
        name="ffn_swiglu",
    )(a, wg, wu)


def _mla_prep_kernel(m_ref, gq_ref, gkv_ref, bf_ref, cos_ref, sin_ref,
                     ql_ref, ckv32_ref, ckv16_ref, kr_ref, *, q_lora, kv_lora, n_forget):
    x = m_ref[...]
    ql_ref[...] = (_rms(x[:, :q_lora]) * gq_ref[...]).astype(BF16)
    ckv = _rms(x[:, q_lora:q_lora + kv_lora]) * gkv_ref[...]
    ckv32_ref[...] = ckv
    ckv16_ref[...] = ckv.astype(BF16)
    t = x[:, q_lora + kv_lora:]
    kr = _rope_tile(t, cos_ref[...], sin_ref[...])
    z = t + bf_ref[...]
    logf = jnp.minimum(z, 0.0) - _softplus_neg_abs(z)
    lane = lax.broadcasted_iota(jnp.int32, t.shape, 1)
    rope_w = LANES // 2
    kr_ref[...] = jnp.where(lane < rope_w, kr, jnp.where(lane < rope_w + n_forget, logf, 0.0))


def _mla_prep(mlat, gq, gkv, bf_tile, cos_t, sin_t, q_lora, kv_lora, n_forget):
    m = mlat.shape[0]
    s = cos_t.shape[0]
    tr = _tile(s, 256, SUBLANES)
    ns = s // tr
    rows = lambda w: pl.BlockSpec((tr, w), lambda i: (i, 0))
    shared = lambda w: pl.BlockSpec((1, w), lambda i: (0, 0))
    table = pl.BlockSpec((tr, LANES), lambda i: (i % ns, 0))
    return pl.pallas_call(
        functools.partial(_mla_prep_kernel, q_lora=q_lora, kv_lora=kv_lora, n_forget=n_forget),
        grid=(m // tr,),
        in_specs=[rows(mlat.shape[1]), shared(q_lora), shared(kv_lora), shared(LANES), table, table],
        out_specs=[rows(q_lora), rows(kv_lora), rows(kv_lora), rows(LANES)],
        out_shape=[
            jax.ShapeDtypeStruct((m, q_lora), BF16),
            jax.ShapeDtypeStruct((m, kv_lora), F32),
            jax.ShapeDtypeStruct((m, kv_lora), BF16),
            jax.ShapeDtypeStruct((m, LANES), F32),
        ],
        compiler_params=_cparams(1),
        name="mla_prep",
    )(mlat, gq.reshape(1, q_lora), gkv.reshape(1, kv_lora), bf_tile, cos_t, sin_t)


def _q_up_kernel(a_ref, w_ref, cos_ref, sin_ref, o_ref, *, n_heads, out_scale):
    acc = _dot(a_ref[...], w_ref[...]) * out_scale
    cos = cos_ref[...]
    sin = sin_ref[...]
    for h in range(n_heads):
        base = h * MLA_QK_PAD
        o_ref[:, base:base + MLA_NOPE] = acc[:, base:base + MLA_NOPE].astype(BF16)
        t = acc[:, base + MLA_NOPE:base + MLA_QK_PAD]
        o_ref[:, base + MLA_NOPE:base + MLA_QK_PAD] = _rope_tile(t, cos, sin).astype(BF16)


def _q_up(qlat, wq, cos_t, sin_t, n_heads, out_scale):
    m, k = qlat.shape
    n = wq.shape[1]
    s = cos_t.shape[0]
    tr = _tile(s, 512, SUBLANES)
    ns = s // tr
    table = pl.BlockSpec((tr, LANES), lambda i: (i % ns, 0))
    return pl.pallas_call(
        functools.partial(_q_up_kernel, n_heads=n_heads, out_scale=out_scale),
        grid=(m // tr,),
        in_specs=[pl.BlockSpec((tr, k), lambda i: (i, 0)), pl.BlockSpec((k, n), lambda i: (0, 0)), table, table],
        out_specs=pl.BlockSpec((tr, n), lambda i: (i, 0)),
        out_shape=jax.ShapeDtypeStruct((m, n), BF16),
        compiler_params=_cparams(1),
        name="mla_q_up",
    )(qlat, wq, cos_t, sin_t)


def _kv_up_kernel(a_ref, wk_ref, wv_ref, kr_ref, k_ref, v_ref, *, n_heads):
    a = a_ref[...].astype(BF16)
    kn = _dot(a, wk_ref[...])
    v_ref[...] = _dot(a, wv_ref[...]).astype(BF16)
    kr = kr_ref[...]
    lane = lax.broadcasted_iota(jnp.int32, kr.shape, 1)
    krm = jnp.where(lane < LANES // 2, kr, 0.0).astype(BF16)
    for h in range(n_heads):
        base = h * MLA_QK_PAD
        k_ref[:, base:base + MLA_NOPE] = kn[:, h * MLA_NOPE:(h + 1) * MLA_NOPE].astype(BF16)
        k_ref[:, base + MLA_NOPE:base + MLA_QK_PAD] = krm


def _kv_up(ckv, wk, wv, kr, n_heads):
    m, k = ckv.shape
    tm = _tile(m, 512, SUBLANES)
    rows = lambda w: pl.BlockSpec((tm, w), lambda i: (i, 0))
    whole = lambda a: pl.BlockSpec(a.shape, lambda i: (0, 0))
    return pl.pallas_call(
        functools.partial(_kv_up_kernel, n_heads=n_heads),
        grid=(m // tm,),
        in_specs=[rows(k), whole(wk), whole(wv), rows(LANES)],
        out_specs=[rows(n_heads * MLA_QK_PAD), rows(n_heads * MLA_V)],
        out_shape=[
            jax.ShapeDtypeStruct((m, n_heads * MLA_QK_PAD), BF16),
            jax.ShapeDtypeStruct((m, n_heads * MLA_V), BF16),
        ],
        compiler_params=_cparams(1),
        name="mla_kv_up",
    )(ckv, wk, wv, kr)


def _split3(x):
    x1 = x.astype(BF16)
    r1 = x - x1.astype(F32)
    x2 = r1.astype(BF16)
    x3 = (r1 - x2.astype(F32)).astype(BF16)
    return x1, x2, x3


def _cumsum_kernel(x_ref, o_ref, *rest, tc, bias_lanes):
    carry_ref = rest[-1]

    @pl.when(pl.program_id(1) == 0)
    def _():
        carry_ref[...] = jnp.zeros_like(carry_ref)

    r = lax.broadcasted_iota(jnp.int32, (tc, tc), 0)
    c = lax.broadcasted_iota(jnp.int32, (tc, tc), 1)
    tri = jnp.where(c <= r, 1.0, 0.0).astype(BF16)
    x1, x2, x3 = _split3(x_ref[0])
    y = (_dot(tri, x3) + _dot(tri, x2)) + _dot(tri, x1) + carry_ref[...]
    o_ref[0] = y
    carry_ref[...] = y[tc - 1:tc, :]
    if bias_lanes is not None:
        kb_ref = rest[0]
        lane0, n_heads = bias_lanes
        parts = [p.astype(F32) for p in _split3(y * (-LOG2E))]
        lane = lax.broadcasted_iota(jnp.int32, (tc, LANES), 1)
        for h in range(n_heads):
            col = [jnp.broadcast_to(p[:, lane0 + h:lane0 + h + 1], (tc, LANES)) for p in parts]
            blk = jnp.where(lane == 0, col[0], jnp.where(lane == 1, col[1], jnp.where(lane == 2, col[2], 0.0)))
            kb_ref[0, :, h * LANES:(h + 1) * LANES] = blk.astype(BF16)


def _cumsum_rows(x, bias_lanes=None):
    b, s, w = x.shape
    tc = _tile(s, 256, 2 * SUBLANES)
    blk = pl.BlockSpec((1, tc, w), lambda bi, j: (bi, j, 0))
    out_specs, out_shape = [blk], [jax.ShapeDtypeStruct((b, s, w), F32)]
    if bias_lanes is not None:
        wide = bias_lanes[1] * LANES
        out_specs.append(pl.BlockSpec((1, tc, wide), lambda bi, j: (bi, j, 0)))
        out_shape.append(jax.ShapeDtypeStruct((b, s, wide), BF16))
    outs = pl.pallas_call(
        functools.partial(_cumsum_kernel, tc=tc, bias_lanes=bias_lanes),
        grid=(b, s // tc),
        in_specs=[blk],
        out_specs=out_specs,
        out_shape=out_shape,
        scratch_shapes=[pltpu.VMEM((1, w), F32)],
        compiler_params=_cparams(2, sequential_axes=(1,)),
        name="cumsum_rows",
    )(x)
    return outs if bias_lanes is not None else outs[0]


def _rep_lanes(x, width):
    return x if width == LANES else jnp.concatenate([x] * (width // LANES), axis=1)


def _softmax_attn_kernel(*refs, t, group, dq, dv, chunk, has_bias):
    if has_bias:
        qt_ref, k_ref, kb_ref, vt_ref, o_ref, m_ref, l_ref, acc_ref, sa_ref, sb_ref = refs
    else:
        qt_ref, k_ref, vt_ref, o_ref, m_ref, l_ref, acc_ref, sa_ref, sb_ref = refs
    i = pl.program_id(2)
    m_ref[...] = jnp.full_like(m_ref, -jnp.inf)
    l_ref[...] = jnp.zeros_like(l_ref)
    acc_ref[...] = jnp.zeros_like(acc_ref)
    if has_bias:
        row = lax.broadcasted_iota(jnp.int32, (LANES, t), 0)
        ones_rows = jnp.where(row < N_BIAS_TERMS, 1.0, 0.0).astype(BF16)

    def score(g, ks):
        k = k_ref[0, pl.ds(ks, t), g * dq:(g + 1) * dq]
        qt = qt_ref[0, g * dq:(g + 1) * dq, :]
        if has_bias:
            k = jnp.concatenate([k, kb_ref[0, pl.ds(ks, t), g * LANES:(g + 1) * LANES]], axis=1)
            qt = jnp.concatenate([qt, ones_rows], axis=0)
        return _dot(k, qt)

    heads = range(group)

    def scores_into(j, s_ref):
        ks = pl.multiple_of(j * t, t)
        for g in heads:
            s_ref[g] = score(g, ks)

    def fold(j, s_ref, masked):
        ks = pl.multiple_of(j * t, t)
        probs, alphas = [], []
        for g in heads:
            s = s_ref[g]
            if masked:
                kpos = lax.broadcasted_iota(jnp.int32, s.shape, 0)
                qpos = lax.broadcasted_iota(jnp.int32, s.shape, 1)
                visible = (kpos // chunk) <= (qpos // chunk) if chunk else kpos <= qpos
                s = jnp.where(visible, s, -jnp.inf)
            m_prev = m_ref[g]
            m_new = jnp.maximum(m_prev, jnp.max(s, axis=0, keepdims=True))
            alpha = jnp.exp2(m_prev - m_new)
            p = jnp.exp2(s - m_new)
            l_ref[g] = alpha * l_ref[g] + jnp.sum(p, axis=0, keepdims=True)
            m_ref[g] = m_new
            probs.append(p.astype(BF16))
            alphas.append(alpha)
        for g in heads:
            pv = _dot(vt_ref[0, g * dv:(g + 1) * dv, pl.ds(ks, t)], probs[g])
            acc_ref[g] = alphas[g] * acc_ref[g] + pv

    scores_into(0, sa_ref)

    def body(jj, carry):
        j = 2 * jj
        scores_into(j + 1, sb_ref)
        fold(j, sa_ref, False)
        scores_into(j + 2, sa_ref)
        fold(j + 1, sb_ref, False)
        return carry

    lax.fori_loop(0, i // 2, body, 0)

    @pl.when(i % 2 == 0)
    def _():
        fold(i, sa_ref, True)

    @pl.when(i % 2 == 1)
    def _():
        scores_into(i, sb_ref)
        fold(i - 1, sa_ref, False)
        fold(i, sb_ref, True)

    for g in range(group):
        o_ref[0, :, g * dv:(g + 1) * dv] = (acc_ref[g] / l_ref[g]).T


def _head_group(n_heads):
    return ATTN_HEAD_GROUP if n_heads % ATTN_HEAD_GROUP == 0 else 1


def _softmax_attn(qt, k, vt, n_heads, dq, dv, chunk, name, kb=None):
    b, s, _ = k.shape
    t = _tile(s, ATTN_TILE, LANES)
    grp = _head_group(n_heads)
    stat = pltpu.VMEM((grp, 1, t), F32)
    keys = lambda w: pl.BlockSpec((1, s, grp * w), lambda bi, h, i: (bi, 0, h))
    in_specs = [pl.BlockSpec((1, grp * dq, t), lambda bi, h, i: (bi, h, i)), keys(dq)]
    args = [qt, k]
    if kb is not None:
        in_specs.append(keys(LANES))
        args.append(kb)
    in_specs.append(pl.BlockSpec((1, grp * dv, s), lambda bi, h, i: (bi, h, 0)))
    args.append(vt)
    return pl.pallas_call(
        functools.partial(_softmax_attn_kernel, t=t, group=grp, dq=dq, dv=dv, chunk=chunk, has_bias=kb is not None),
        grid=(b, n_heads // grp, s // t),
        in_specs=in_specs,
        out_specs=pl.BlockSpec((1, t, grp * dv), lambda bi, h, i: (bi, i, h)),
        out_shape=jax.ShapeDtypeStruct((b, s, n_heads * dv), F32),
        scratch_shapes=[stat, stat, pltpu.VMEM((grp, dv, t), F32),
                        pltpu.VMEM((grp, t, t), F32), pltpu.VMEM((grp, t, t), F32)],
        compiler_params=_cparams(3),
        name=name,
    )(*args)


def _softmax_attn_cached_kernel(*refs, has_bias, chunk, past):
    if has_bias:
        q_ref, kc_ref, vc_ref, kn_ref, vn_ref, fkc_ref, fkn_ref, o_ref = refs
    else:
        q_ref, kc_ref, vc_ref, kn_ref, vn_ref, o_ref = refs
    q = q_ref[...]
    s_c = _dot_nt(q, kc_ref[...].astype(BF16))
    s_n = _dot_nt(q, kn_ref[...])
    if has_bias:
        s_c = s_c - fkc_ref[...] * LOG2E
        s_n = s_n - fkn_ref[...] * LOG2E
    r = lax.broadcasted_iota(jnp.int32, s_n.shape, 0) + past
    c = lax.broadcasted_iota(jnp.int32, s_n.shape, 1) + past
    visible = (c // chunk) <= (r // chunk) if chunk else c <= r
    s_n = jnp.where(visible, s_n, -jnp.inf)
    m = jnp.maximum(jnp.max(s_c, axis=-1, keepdims=True), jnp.max(s_n, axis=-1, keepdims=True))
    p_c = jnp.exp2(s_c - m)
    p_n = jnp.exp2(s_n - m)
    l = jnp.sum(p_c, axis=-1, keepdims=True) + jnp.sum(p_n, axis=-1, keepdims=True)
    acc = _dot(p_c.astype(BF16), vc_ref[...].astype(BF16)) + _dot(p_n.astype(BF16), vn_ref[...])
    o_ref[...] = acc / l


def _cache_spec(cache, layer, d):
    if layer is None:
        return pl.BlockSpec((None, cache.shape[1], d), lambda bi, h: (bi, 0, h))
    return pl.BlockSpec((None, None, cache.shape[2], d), lambda bi, h: (layer, bi, 0, h))


def _softmax_attn_cached(q, kc, vc, kn, vn, n_heads, dq, dv, chunk, f_row_c=None, f_row_n=None, layer=None):
    b, sn, _ = q.shape
    past = kc.shape[-2]
    has_bias = f_row_c is not None
    new = lambda d: pl.BlockSpec((None, sn, d), lambda bi, h: (bi, 0, h))
    in_specs = [new(dq), _cache_spec(kc, layer, dq), _cache_spec(vc, layer, dv), new(dq), new(dv)]
    args = [q, kc, vc, kn, vn]
    if has_bias:
        in_specs += [
            pl.BlockSpec((None, None, 1, past), lambda bi, h: (bi, h, 0, 0)),
            pl.BlockSpec((None, None, 1, sn), lambda bi, h: (bi, h, 0, 0)),
        ]
        args += [f_row_c, f_row_n]
    return pl.pallas_call(
        functools.partial(_softmax_attn_cached_kernel, has_bias=has_bias, chunk=chunk, past=past),
        grid=(b, n_heads),
        in_specs=in_specs,
        out_specs=new(dv),
        out_shape=jax.ShapeDtypeStruct((b, sn, n_heads * dv), F32),
        compiler_params=_cparams(2),
        name="fox_attn_cached" if has_bias else "mla_attn_cached",
    )(*args)


def _cache_block(layer, tp, n_heads, d, order):
    return pl.BlockSpec((None, None, tp, n_heads, d), lambda bi, j: (layer, bi, order(j), 0, 0))


def _fox_cached_kernel(q_ref, kc_ref, vc_ref, kn_ref, vn_ref, fkc_ref, fkn_ref, o_ref, m_ref, l_ref, acc_ref,
                       *, n_heads, d):
    j = pl.program_id(1)

    def update(h, s, v):
        m_prev = m_ref[h]
        m_new = jnp.maximum(m_prev, jnp.max(s, axis=-1, keepdims=True))
        alpha = jnp.exp2(m_prev - m_new)
        p = jnp.exp2(s - m_new)
        l_ref[h] = alpha * l_ref[h] + jnp.sum(p, axis=-1, keepdims=True)
        acc_ref[h] = alpha * acc_ref[h] + _dot(p.astype(BF16), v)
        m_ref[h] = m_new

    @pl.when(j == 0)
    def _():
        m_ref[...] = jnp.full_like(m_ref, -jnp.inf)
        l_ref[...] = jnp.zeros_like(l_ref)
        acc_ref[...] = jnp.zeros_like(acc_ref)
        for h in range(n_heads):
            cols = slice(h * d, (h + 1) * d)
            s = _dot_nt(q_ref[:, cols], kn_ref[:, cols]) - fkn_ref[h] * LOG2E
            r = lax.broadcasted_iota(jnp.int32, s.shape, 0)
            c = lax.broadcasted_iota(jnp.int32, s.shape, 1)
            update(h, jnp.where(c <= r, s, -jnp.inf), vn_ref[:, cols])

    for h in range(n_heads):
        s = _dot_nt(q_ref[:, h * d:(h + 1) * d], kc_ref[:, h, :].astype(BF16)) - fkc_ref[h] * LOG2E
        update(h, s, vc_ref[:, h, :].astype(BF16))

    @pl.when(j == pl.num_programs(1) - 1)
    def _():
        for h in range(n_heads):
            o_ref[:, h * d:(h + 1) * d] = acc_ref[h] / l_ref[h]


def _fox_attn_cached(q, kc, vc, kn, vn, f_row_c, f_row_n, layer):
    b, sn, _ = q.shape
    _, _, past, n_heads, d = kc.shape
    tp = _tile(past, CACHE_BLOCK, LANES)
    new = pl.BlockSpec((None, sn, n_heads * d), lambda bi, j: (bi, 0, 0))
    cache = _cache_block(layer, tp, n_heads, d, lambda j: j)
    stat = pltpu.VMEM((n_heads, sn, 1), F32)
    return pl.pallas_call(
        functools.partial(_fox_cached_kernel, n_heads=n_heads, d=d),
        grid=(b, past // tp),
        in_specs=[new, cache, cache, new, new,
                  pl.BlockSpec((None, n_heads, 1, tp), lambda bi, j: (bi, 0, 0, j)),
                  pl.BlockSpec((None, n_heads, 1, sn), lambda bi, j: (bi, 0, 0, 0))],
        out_specs=new,
        out_shape=jax.ShapeDtypeStruct((b, sn, n_heads * d), F32),
        scratch_shapes=[stat, stat, pltpu.VMEM((n_heads, sn, d), F32)],
        compiler_params=_cparams(2, sequential_axes=(1,)),
        name="fox_attn_cached",
    )(q, kc, vc, kn, vn, f_row_c, f_row_n)


def _suffix_matrix(t):
    r = lax.broadcasted_iota(jnp.int32, (t, t), 0)
    c = lax.broadcasted_iota(jnp.int32, (t, t), 1)
    return jnp.where(r >= c, 1.0, 0.0).astype(BF16)


def _sb_fold(q, k, v, u, c, strict):
    z = _dot_nt(q, k)
    neg_abs = lax.bitcast_convert_type(lax.bitcast_convert_type(z, jnp.uint32) | jnp.uint32(0x80000000), F32)
    sp = jnp.maximum(z, 0.0) + jnp.log(1.0 + jnp.exp2(neg_abs)) * LOG2E
    if strict is not None:
        sp = jnp.where(strict, sp, 0.0)
    hi = lax.bitcast_convert_type(lax.bitcast_convert_type(sp, jnp.uint32) & jnp.uint32(0xFFFF0000), F32)
    lo = (sp - hi).astype(BF16)
    suffix = _dot(lo, u) + _dot(hi.astype(BF16), u)
    width = z.shape[1]
    c_wide = c[:, :width] if width < LANES else _rep_lanes(c, width)
    a = jnp.exp2(z - (suffix + c_wide))
    if strict is not None:
        a = jnp.where(strict, a, 0.0)
    return _dot(a.astype(BF16), v), c + jnp.sum(sp, axis=-1, keepdims=True)


def _sb_attn_kernel(qt_ref, k_ref, vt_ref, o_ref, c_ref, acc_ref, *bufs, t, tsub, group, d):
    i = pl.program_id(2)
    buf_a, buf_b = bufs[:3], bufs[3:]
    r = lax.broadcasted_iota(jnp.int32, (tsub, tsub), 0)
    col = lax.broadcasted_iota(jnp.int32, (tsub, tsub), 1)
    ut = jnp.where(col >= r, 1.0, 0.0).astype(BF16)
    c_ref[...] = jnp.zeros_like(c_ref)
    acc_ref[...] = jnp.zeros_like(acc_ref)
    n_sub = t // tsub
    chains = [(g, sub) for g in range(group) for sub in reversed(range(n_sub))]

    def strict_mask(sub):
        kpos = lax.broadcasted_iota(jnp.int32, (tsub, t), 0) + sub * tsub
        qpos = lax.broadcasted_iota(jnp.int32, (tsub, t), 1)
        return kpos < qpos

    def scores(j, buf, masked):
        z_ref, hi_ref, lo_ref = buf
        z = {}
        for g, sub in chains:
            start = pl.multiple_of(j * t + sub * tsub, tsub)
            z[g, sub] = _dot(k_ref[0, pl.ds(start, tsub), g * d:(g + 1) * d], qt_ref[0, g * d:(g + 1) * d, :])
        for g, sub in chains:
            zc = z[g, sub]
            neg_abs = lax.bitcast_convert_type(lax.bitcast_convert_type(zc, jnp.uint32) | jnp.uint32(0x80000000), F32)
            sp = jnp.maximum(zc, 0.0) + jnp.log(1.0 + jnp.exp2(neg_abs)) * LOG2E
            if masked:
                sp = jnp.where(strict_mask(sub), sp, 0.0)
            hi = lax.bitcast_convert_type(lax.bitcast_convert_type(sp, jnp.uint32) & jnp.uint32(0xFFFF0000), F32)
            z_ref[g, sub] = zc
            hi_ref[g, sub] = hi.astype(BF16)
            lo_ref[g, sub] = (sp - hi).astype(BF16)

    def fold(j, buf, masked):
        z_ref, hi_ref, lo_ref = buf
        suffix = {(g, sub): _dot(ut, lo_ref[g, sub]) + _dot(ut, hi_ref[g, sub]) for g, sub in chains}
        c = {g: c_ref[g] for g in range(group)}
        acc = {g: acc_ref[g] for g in range(group)}
        for g, sub in chains:
            start = pl.multiple_of(j * t + sub * tsub, tsub)
            a = jnp.exp2(z_ref[g, sub] - (suffix[g, sub] + c[g]))
            if masked:
                a = jnp.where(strict_mask(sub), a, 0.0)
            acc[g] = acc[g] + _dot(vt_ref[0, g * d:(g + 1) * d, pl.ds(start, tsub)], a.astype(BF16))
            c[g] = c[g] + suffix[g, sub][0:1, :]
        for g in range(group):
            c_ref[g] = c[g]
            acc_ref[g] = acc[g]

    scores(i, buf_a, True)
    fold(i, buf_a, True)
    block = lambda n: jnp.maximum(i - 1 - n, 0)
    scores(block(0), buf_a, False)

    def body(jj, carry):
        n = 2 * jj
        scores(block(n + 1), buf_b, False)
        fold(block(n), buf_a, False)
        scores(block(n + 2), buf_a, False)
        fold(block(n + 1), buf_b, False)
        return carry

    lax.fori_loop(0, i // 2, body, 0)

    @pl.when(i % 2 == 1)
    def _():
        fold(0, buf_a, False)

    for g in range(group):
        o_ref[0, :, g * d:(g + 1) * d] = acc_ref[g].T


def _sb_attn(qt, k, vt, n_heads):
    b, s, _ = k.shape
    d = HEAD_DIM
    t = _tile(s, ATTN_TILE, LANES)
    tsub = _tile(t, SB_SUB_TILE, LANES)
    grp = _head_group(n_heads)
    return pl.pallas_call(
        functools.partial(_sb_attn_kernel, t=t, tsub=tsub, group=grp, d=d),
        grid=(b, n_heads // grp, s // t),
        in_specs=[
            pl.BlockSpec((1, grp * d, t), lambda bi, h, i: (bi, h, i)),
            pl.BlockSpec((1, s, grp * d), lambda bi, h, i: (bi, 0, h)),
            pl.BlockSpec((1, grp * d, s), lambda bi, h, i: (bi, h, 0)),
        ],
        out_specs=pl.BlockSpec((1, t, grp * d), lambda bi, h, i: (bi, i, h)),
        out_shape=jax.ShapeDtypeStruct((b, s, n_heads * d), F32),
        scratch_shapes=[pltpu.VMEM((grp, 1, t), F32), pltpu.VMEM((grp, d, t), F32)]
        + [pltpu.VMEM((grp, t // tsub, tsub, t), dt) for dt in (F32, BF16, BF16)] * 2,
        compiler_params=_cparams(3),
        name="sb_attn",
    )(qt, k, vt)


def _sb_cached_kernel(q_ref, kc_ref, vc_ref, kn_ref, vn_ref, o_ref, c_ref, acc_ref, *, n_heads, d, tk):
    j = pl.program_id(1)
    sn = q_ref.shape[0]

    @pl.when(j == 0)
    def _():
        r = lax.broadcasted_iota(jnp.int32, (sn, sn), 0)
        col = lax.broadcasted_iota(jnp.int32, (sn, sn), 1)
        u_new = _suffix_matrix(sn)
        for h in range(n_heads):
            cols = slice(h * d, (h + 1) * d)
            acc_ref[h], c_ref[h] = _sb_fold(q_ref[:, cols], kn_ref[:, cols], vn_ref[:, cols], u_new,
                                            jnp.zeros((sn, LANES), F32), col < r)

    u = _suffix_matrix(tk)
    for h in range(n_heads):
        q = q_ref[:, h * d:(h + 1) * d]
        c = c_ref[h]
        acc = acc_ref[h]
        for blk in reversed(range(kc_ref.shape[0] // tk)):
            rows = slice(blk * tk, (blk + 1) * tk)
            pv, c = _sb_fold(q, kc_ref[rows, h, :].astype(BF16), vc_ref[rows, h, :].astype(BF16), u, c, None)
            acc = acc + pv
        c_ref[h] = c
        acc_ref[h] = acc

    @pl.when(j == pl.num_programs(1) - 1)
    def _():
        for h in range(n_heads):
            o_ref[:, h * d:(h + 1) * d] = acc_ref[h]


def _sb_attn_cached(q, kc, vc, kn, vn, layer):
    b, sn, _ = q.shape
    _, _, past, n_heads, d = kc.shape
    tp = _tile(past, CACHE_BLOCK, LANES)
    tk = _tile(tp, SB_SUB_TILE, LANES)
    n_blocks = past // tp
    new = pl.BlockSpec((None, sn, n_heads * d), lambda bi, j: (bi, 0, 0))
    cache = _cache_block(layer, tp, n_heads, d, lambda j: n_blocks - 1 - j)
    return pl.pallas_call(
        functools.partial(_sb_cached_kernel, n_heads=n_heads, d=d, tk=tk),
        grid=(b, n_blocks),
        in_specs=[new, cache, cache, new, new],
        out_specs=new,
        out_shape=jax.ShapeDtypeStruct((b, sn, n_heads * d), F32),
        scratch_shapes=[pltpu.VMEM((n_heads, sn, LANES), F32), pltpu.VMEM((n_heads, sn, d), F32)],
        compiler_params=_cparams(2, sequential_axes=(1,)),
        name="sb_attn_cached",
    )(q, kc, vc, kn, vn)


def _group_norm_kernel(of_ref, os_ref, om_ref, g_ref, h_ref):
    off = 0
    for ref in (of_ref, os_ref, om_ref):
        w = ref.shape[-1]
        h_ref[0, :, off:off + w] = (_rms(ref[0]) * g_ref[:, off:off + w]).astype(BF16)
        off += w


def _group_norm(o_fox, o_sb, o_mla, g):
    b, s, _ = o_fox.shape
    d = g.shape[-1]
    tr = _tile(s, 256, SUBLANES)
    rows = lambda a: pl.BlockSpec((1, tr, a.shape[-1]), lambda bi, i: (bi, i, 0))
    return pl.pallas_call(
        _group_norm_kernel,
        grid=(b, s // tr),
        in_specs=[rows(o_fox), rows(o_sb), rows(o_mla), pl.BlockSpec((1, d), lambda bi, i: (0, 0))],
        out_specs=pl.BlockSpec((1, tr, d), lambda bi, i: (bi, i, 0)),
        out_shape=jax.ShapeDtypeStruct((b, s, d), BF16),
        compiler_params=_cparams(2),
        name="group_norm",
    )(o_fox, o_sb, o_mla, g.reshape(1, d))


def _rope_tables(pos, rope_dim):
    half = rope_dim // 2
    inv = ROPE_THETA ** (-jnp.arange(half, dtype=F32) / half)
    ang = pos.astype(F32)[:, None] * inv[None, :]
    cos, sin = jnp.cos(ang), jnp.sin(ang)
    n = pos.shape[0]
    pad = LANES - rope_dim
    cos_t = jnp.concatenate([cos, cos, jnp.ones((n, pad), F32)], axis=-1)
    sin_t = jnp.concatenate([-sin, sin, jnp.zeros((n, pad), F32)], axis=-1)
    return cos_t, sin_t


def _prep_layer_weights(l, dims, w_in, b_forget, w_q_up, w_kv_up):
    hf, hs, hm, q_lora, kv_lora, rope = dims
    wf, ws = hf * HEAD_DIM, hs * HEAD_DIM
    d = w_in.shape[1]
    w = w_in[l]
    widths = (wf, wf, wf, hf, ws, ws, ws, q_lora, kv_lora, rope)
    offs = [0]
    for x in widths:
        offs.append(offs[-1] + x)
    col = lambda i: w[:, offs[i]:offs[i + 1]]
    lw = {"fox_cols": (offs[0], offs[1], offs[2])}
    for name, i in (("sq", 4), ("sk", 5), ("sv", 6)):
        lw[name] = col(i).astype(BF16)
    pad = jnp.zeros((d, LANES - rope - hf), F32)
    lw["mla_in"] = jnp.concatenate([col(7), col(8), col(9), col(3), pad], axis=1).astype(BF16)
    lw["bf"] = jnp.zeros((1, LANES), F32).at[0, rope:rope + hf].set(b_forget[l])
    wq = w_q_up[l].reshape(q_lora, hm, MLA_NOPE + rope)
    wq = jnp.pad(wq, ((0, 0), (0, 0), (0, MLA_QK_PAD - MLA_NOPE - rope)))
    lw["wq"] = wq.reshape(q_lora, hm * MLA_QK_PAD).astype(BF16)
    wkv = w_kv_up[l].reshape(kv_lora, hm, MLA_NOPE + MLA_V)
    lw["wk"] = wkv[:, :, :MLA_NOPE].reshape(kv_lora, hm * MLA_NOPE).astype(BF16)
    lw["wv"] = wkv[:, :, MLA_NOPE:].reshape(kv_lora, hm * MLA_V).astype(BF16)
    return lw


def _ffn(h, gw, l, f):
    b, s, d = h.shape
    a = _swiglu(h.reshape(b * s, d), gw["wg"], gw["wu"], (l, f))
    (y,) = _mm(a, gw["wd"], (F32,), "ffn_down", w_index=(l, f), tm_pref=512, tn_pref=512)
    return y.reshape(b, s, d)


def _token_mix(h, l, lw, gw, dims, tables, cache, stacked):
    hf, hs, hm, q_lora, kv_lora, rope = dims
    b, s, d = h.shape
    h2 = h.reshape(b * s, d)
    cos_t, sin_t = tables
    three = lambda a: a.reshape(b, s, -1)
    depth = gw["w_in"].shape[0]

    qk_scale = HEAD_DIM ** -0.5 * LOG2E
    mla_scale = (MLA_NOPE + rope) ** -0.5 * LOG2E
    fq_col, fk_col, fv_col = lw["fox_cols"]
    wf = hf * HEAD_DIM
    stacked = dict(stacked)
    (fq,) = _mm(h2, gw["w_in"], (BF16,), "in_fox_q", w_index=(l,), col0=fq_col, n=wf, out_scale=qk_scale)
    stacked["fk"], fk16 = _mm_state(h2, gw["w_in"], (l,), fk_col, hf, l, stacked["fk"], depth, "in_fox_k")
    stacked["fv"], fv16 = _mm_state(h2, gw["w_in"], (l,), fv_col, hf, l, stacked["fv"], depth, "in_fox_v")
    (sq,) = _mm(h2, lw["sq"], (BF16,), "in_sb_q", out_scale=qk_scale)
    stacked["sk"], sk16 = _mm_state(h2, lw["sk"], (), 0, hs, l, stacked["sk"], depth, "in_sb_k")
    stacked["sv"], sv16 = _mm_state(h2, lw["sv"], (), 0, hs, l, stacked["sv"], depth, "in_sb_v")
    (mlat,) = _mm(h2, lw["mla_in"], (F32,), "in_mla", tm_pref=512, tn_pref=lw["mla_in"].shape[1])

    qlat, ckv32, ckv16, krlf = _mla_prep(mlat, lw["gq"], lw["gkv"], lw["bf"], cos_t, sin_t, q_lora, kv_lora, hf)
    q_mla = _q_up(qlat, lw["wq"], cos_t, sin_t, hm, mla_scale)
    k_mla, v_mla = _kv_up(ckv16, lw["wk"], lw["wv"], krlf, hm)

    krlf3 = three(krlf)
    krope = krlf3[:, :, :rope]
    logf = krlf3[:, :, rope:rope + hf]

    if cache is None:
        transposed = lambda a: jnp.swapaxes(three(a), 1, 2)
        _, k_bias = _cumsum_rows(krlf3, bias_lanes=(rope, hf))
        o_fox = _softmax_attn(transposed(fq), three(fk16), transposed(fv16), hf, HEAD_DIM, HEAD_DIM, 0,
                              "fox_attn", kb=k_bias)
        o_sb = _sb_attn(transposed(sq), three(sk16), transposed(sv16), hs)
        o_mla = _softmax_attn(transposed(q_mla), three(k_mla), transposed(v_mla), hm, MLA_QK_PAD, MLA_V, CHUNK, "mla_attn")
    else:
        c_fk, c_fv, c_logf, c_sk, c_sv, c_ckv, c_krope = cache
        past = c_fk.shape[2]
        flat = lambda a: a.reshape(b, past, -1)
        logf_all = jnp.concatenate([c_logf[l].astype(F32), logf], axis=1)
        logf_all = jnp.pad(logf_all, ((0, 0), (0, 0), (0, LANES - hf)))
        f_all = _cumsum_rows(logf_all)[:, :, :hf]
        f_bh = jnp.swapaxes(f_all, 1, 2)
        o_fox = _fox_attn_cached(three(fq), c_fk, c_fv, three(fk16), three(fv16),
                                 f_bh[:, :, None, :past], f_bh[:, :, None, past:], l)
        o_sb = _sb_attn_cached(three(sq), c_sk, c_sv, three(sk16), three(sv16), l)
        c_kr = jnp.pad(c_krope[l].reshape(b * past, rope), ((0, 0), (0, LANES - rope)))
        kc_mla, vc_mla = _kv_up(c_ckv[l].reshape(b * past, kv_lora), lw["wk"], lw["wv"], c_kr, hm)
        o_mla = _softmax_attn_cached(
            three(q_mla), flat(kc_mla), flat(vc_mla), three(k_mla), three(v_mla), hm, MLA_QK_PAD, MLA_V, CHUNK)

    hmix = _group_norm(o_fox, o_sb, o_mla, lw["g_mix"])
    (mix,) = _mm(hmix.reshape(b * s, d), gw["w_out"], (F32,), "mix_out", w_index=(l,))
    return mix.reshape(b, s, d), (logf, ckv32.reshape(b, s, kv_lora), krope), stacked


def kernel(x_prompt, x_sample, c_prompt, c_sample, cache_fox_k, cache_fox_v, cache_fox_logf, cache_sb_k, cache_sb_v, cache_mla_ckv, cache_mla_krope, w_ada, b_ada, g_norm, w_in, b_forget, g_q_lat, w_q_up, g_kv_lat, w_kv_up, g_mix_out, w_out, w_ffn_gate, w_ffn_up, w_ffn_down, g_final):
    depth = w_ada.shape[0]
    d = x_prompt.shape[-1]
    nbp, nbs = x_prompt.shape[0], x_sample.shape[0]
    past = cache_fox_k.shape[2]
    hf, hs = cache_fox_k.shape[3], cache_sb_k.shape[3]
    kv_lora, rope, q_lora = cache_mla_ckv.shape[-1], cache_mla_krope.shape[-1], g_q_lat.shape[-1]
    hm = w_q_up.shape[-1] // (MLA_NOPE + rope)
    dims = (hf, hs, hm, q_lora, kv_lora, rope)

    c_all = jnp.concatenate([c_prompt, c_sample], axis=0)
    nb = nbp + nbs
    nb_pad = -(-nb // SUBLANES) * SUBLANES
    c_all = jnp.pad(c_all, ((0, nb_pad - nb), (0, 0)))
    mods = _ada(c_all, w_ada, b_ada).reshape(depth, nb_pad, N_MOD, 1, d)
    groups = {"p": (0, nbp), "s": (nbp, nb)}
    mod = lambda l, grp, i: mods[l, groups[grp][0]:groups[grp][1], i]

    tables = {
        "p": _rope_tables(jnp.arange(x_prompt.shape[1]), rope),
        "s": _rope_tables(past + jnp.arange(x_sample.shape[1]), rope),
    }
    caches = (cache_fox_k, cache_fox_v, cache_fox_logf, cache_sb_k, cache_sb_v, cache_mla_ckv, cache_mla_krope)

    gw = {"w_in": w_in.astype(BF16), "w_out": w_out.astype(BF16), "wg": w_ffn_gate.astype(BF16),
          "wu": w_ffn_up.astype(BF16), "wd": w_ffn_down.astype(BF16)}

    xs = {"p": x_prompt, "s": x_sample}
    pending = {"p": None, "s": None}
    small = {"p": [], "s": []}
    stacked = {grp: {"fk": None, "fv": None, "sk": None, "sv": None} for grp in ("p", "s")}
    for l in range(depth):
        lw = _prep_layer_weights(l, dims, w_in, b_forget, w_q_up, w_kv_up)
        lw["gq"], lw["gkv"], lw["g_mix"] = g_q_lat[l], g_kv_lat[l], g_mix_out[l]
        for grp in ("p", "s"):
            m = lambda i: mod(l, grp, i)
            x, h = _prenorm(xs[grp], g_norm[l, 0], m(1), m(0), res=pending[grp])
            y = _ffn(h, gw, l, 0)
            x, h = _prenorm(x, g_norm[l, 1], m(4), m(3), res=(y, m(2), 0.5))
            cache = None if grp == "p" else caches
            mix, st, stacked[grp] = _token_mix(h, l, lw, gw, dims, tables[grp], cache, stacked[grp])
            small[grp].append(st)
            x, h = _prenorm(x, g_norm[l, 2], m(7), m(6), res=(mix, m(5), 1.0))
            y = _ffn(h, gw, l, 1)
            xs[grp] = x
            pending[grp] = (y, m(8), 0.5)

    y_prompt = _prenorm(xs["p"], g_final, res=pending["p"], final=True)
    y_sample = _prenorm(xs["s"], g_final, res=pending["s"], final=True)

    def states(grp, x):
        bsz, seq = x.shape[0], x.shape[1]
        big = lambda name: stacked[grp][name].reshape((depth, bsz, seq) + stacked[grp][name].shape[2:])
        stack = lambda i: jnp.stack([st[i] for st in small[grp]])
        return (big("fk"), big("fv"), stack(0), big("sk"), big("sv"), stack(1), stack(2))

    return (y_prompt, y_sample) + states("p", x_prompt) + states("s", x_sample)
```

```python
import functools

import jax
import jax.numpy as jnp
from jax import lax
from jax.experimental import pallas as pl
from jax.experimental.pallas import tpu as pltpu

F32 = jnp.float32
BF16 = jnp.bfloat16

HEAD_DIM = 128
MLA_NOPE = 128
MLA_V = 128
CHUNK = 64
N_MOD = 9
EPS = 1e-6
ROPE_THETA = 10000.0

LANES = 128
SUBLANES = 8
MLA_QK_PAD = 256
VMEM_LIMIT = 56 * 1024 * 1024
LOG2E = 1.4426950408889634

ATTN_TILE = 512
ATTN_HEAD_GROUP = 2
N_BIAS_TERMS = 3
CACHE_BLOCK = 1024
SB_SUB_TILE = 256


def _cparams(n_axes, sequential_axes=()):
    sem = tuple("arbitrary" if a in sequential_axes else "parallel" for a in range(n_axes))
    return pltpu.CompilerParams(dimension_semantics=sem, vmem_limit_bytes=VMEM_LIMIT)


def _tile(n, pref, align):
    if n <= pref:
        return n
    for t in range(pref - pref % align, 0, -align):
        if n % t == 0:
            return t
    return n


def _dot(a, b):
    return jnp.dot(a, b, preferred_element_type=F32)


def _dot_nt(a, b):
    return lax.dot_general(a, b, (((1,), (1,)), ((), ())), preferred_element_type=F32)


def _softplus_neg_abs(z):
    return jnp.log1p(jnp.exp(-jnp.abs(z)))


def _rms(x):
    return x * lax.rsqrt(jnp.mean(x * x, axis=-1, keepdims=True) + EPS)


def _rope_tile(t, cos, sin):
    lane = lax.broadcasted_iota(jnp.int32, t.shape, 1)
    half = LANES // 4
    partner = jnp.where(lane < half, pltpu.roll(t, LANES - half, 1), pltpu.roll(t, half, 1))
    return t * cos + partner * sin


def _ada_kernel(c_ref, w_ref, b_ref, o_ref):
    c = c_ref[...]
    a = (c / (1.0 + jnp.exp(-c))).astype(BF16)
    o_ref[0] = _dot(a, w_ref[0].astype(BF16)) + b_ref[0]


def _ada(c_all, w_ada, b_ada):
    depth, d, n = w_ada.shape
    nb = c_all.shape[0]
    tn = _tile(n, 512, LANES)
    return pl.pallas_call(
        _ada_kernel,
        grid=(depth, n // tn),
        in_specs=[
            pl.BlockSpec((nb, d), lambda l, j: (0, 0)),
            pl.BlockSpec((1, d, tn), lambda l, j: (l, 0, j)),
            pl.BlockSpec((1, 1, tn), lambda l, j: (l, 0, j)),
        ],
        out_specs=pl.BlockSpec((1, nb, tn), lambda l, j: (l, 0, j)),
        out_shape=jax.ShapeDtypeStruct((depth, nb, n), F32),
        compiler_params=_cparams(2),
        name="ada",
    )(c_all, w_ada, b_ada.reshape(depth, 1, n))


def _prenorm_kernel(*refs, coef, has_res, final):
    refs = list(refs)
    x_ref = refs.pop(0)
    x = x_ref[0]
    if has_res:
        y_ref = refs.pop(0)
        gate_ref = refs.pop(0)
        x = x + (coef * gate_ref[0]) * y_ref[0]
    g_ref = refs.pop(0)
    if final:
        (o_ref,) = refs
        o_ref[0] = _rms(x) * g_ref[...]
        return
    sc_ref = refs.pop(0)
    sh_ref = refs.pop(0)
    if has_res:
        xo_ref = refs.pop(0)
        xo_ref[0] = x
    (h_ref,) = refs
    h = (_rms(x) * g_ref[...]) * (1.0 + sc_ref[0]) + sh_ref[0]
    h_ref[0] = h.astype(BF16)


def _prenorm(x, g, scale=None, shift=None, res=None, final=False):
    b, s, d = x.shape
    tr = _tile(s, 256, SUBLANES)
    row = pl.BlockSpec((1, tr, d), lambda bi, i: (bi, i, 0))
    per_b = pl.BlockSpec((1, 1, d), lambda bi, i: (bi, 0, 0))
    shared = pl.BlockSpec((1, d), lambda bi, i: (0, 0))
    args, specs = [x], [row]
    coef = 0.0
    if res is not None:
        y, gate, coef = res
        args += [y, gate]
        specs += [row, per_b]
    args.append(g.reshape(1, d))
    specs.append(shared)
    out_shape, out_specs = [], []
    if final:
        out_shape.append(jax.ShapeDtypeStruct((b, s, d), F32))
        out_specs.append(row)
    else:
        args += [scale, shift]
        specs += [per_b, per_b]
        if res is not None:
            out_shape.append(jax.ShapeDtypeStruct((b, s, d), F32))
            out_specs.append(row)
        out_shape.append(jax.ShapeDtypeStruct((b, s, d), BF16))
        out_specs.append(row)
    outs = pl.pallas_call(
        functools.partial(_prenorm_kernel, coef=coef, has_res=res is not None, final=final),
        grid=(b, s // tr),
        in_specs=specs,
        out_specs=out_specs,
        out_shape=out_shape,
        compiler_params=_cparams(2),
        name="final_norm" if final else "prenorm",
    )(*args)
    if final:
        return outs[0]
    if res is None:
        return x, outs[0]
    return outs[0], outs[1]


def _weight_spec(w, w_index, k, tn, col_block0=0):
    lead = tuple(w_index)
    return pl.BlockSpec((None,) * len(lead) + (k, tn), lambda i, j: lead + (0, col_block0 + j))


def _mm_kernel(a_ref, w_ref, *o_refs, out_scale):
    acc = _dot(a_ref[...].astype(BF16), w_ref[...])
    if out_scale != 1.0:
        acc = acc * out_scale
    for o_ref in o_refs:
        o_ref[...] = acc.astype(o_ref.dtype)


def _mm(a, w, out_dtypes, name, w_index=(), col0=0, n=None, tm_pref=1024, tn_pref=512, out_scale=1.0):
    m, k = a.shape
    n = w.shape[-1] if n is None else n
    tm = _tile(m, tm_pref, SUBLANES)
    tn = _tile(n, tn_pref, LANES)
    assert col0 % tn == 0
    outs = pl.pallas_call(
        functools.partial(_mm_kernel, out_scale=out_scale),
        grid=(m // tm, n // tn),
        in_specs=[pl.BlockSpec((tm, k), lambda i, j: (i, 0)), _weight_spec(w, w_index, k, tn, col0 // tn)],
        out_specs=[pl.BlockSpec((tm, tn), lambda i, j: (i, j)) for _ in out_dtypes],
        out_shape=[jax.ShapeDtypeStruct((m, n), dt) for dt in out_dtypes],
        compiler_params=_cparams(2),
        name=name,
    )(a, w)
    return outs


def _mm_state_kernel(a_ref, w_ref, o32_ref, o16_ref, *, n_heads):
    acc = _dot(a_ref[...], w_ref[...])
    o16_ref[...] = acc.astype(BF16)
    hd = acc.shape[1] // n_heads
    for h in range(n_heads):
        o32_ref[:, h, :] = acc[:, h * hd:(h + 1) * hd]


def _mm_state(a, w, w_index, col0, n_heads, name, tm_pref=512):
    m, k = a.shape
    hd = HEAD_DIM
    n = n_heads * hd
    tm = _tile(m, tm_pref, 2 * SUBLANES)
    assert col0 % n == 0
    return pl.pallas_call(
        functools.partial(_mm_state_kernel, n_heads=n_heads),
        grid=(m // tm,),
        in_specs=[pl.BlockSpec((tm, k), lambda i: (i, 0)),
                  pl.BlockSpec((None,) * len(w_index) + (k, n), lambda i: tuple(w_index) + (0, col0 // n))],
        out_specs=[pl.BlockSpec((tm, n_heads, hd), lambda i: (i, 0, 0)), pl.BlockSpec((tm, n), lambda i: (i, 0))],
        out_shape=[jax.ShapeDtypeStruct((m, n_heads, hd), F32), jax.ShapeDtypeStruct((m, n), BF16)],
        compiler_params=_cparams(1),
        name=name,
    )(a, w)


def _swiglu_kernel(a_ref, wg_ref, wu_ref, o_ref, *, rows):
    for r0 in range(0, a_ref.shape[0], rows):
        a = a_ref[r0:r0 + rows, :]
        g = _dot(a, wg_ref[...])
        u = _dot(a, wu_ref[...])
        o_ref[r0:r0 + rows, :] = ((g / (1.0 + jnp.exp(-g))) * u).astype(o_ref.dtype)


def _swiglu(a, wg, wu, w_index, tm_pref=2048, tn_pref=256, rows_pref=512):
    m, k = a.shape
    n = wg.shape[-1]
    tm = _tile(m, tm_pref, SUBLANES)
    tn = _tile(n, tn_pref, LANES)
    rows = _tile(tm, rows_pref, SUBLANES)
    wspec = _weight_spec(wg, w_index, k, tn)
    return pl.pallas_call(
        functools.partial(_swiglu_kernel, rows=rows),
        grid=(m // tm, n // tn),
        in_specs=[pl.BlockSpec((tm, k), lambda i, j: (i, 0)), wspec, wspec],
        out_specs=pl.BlockSpec((tm, tn), lambda i, j: (i, j)),
        out_shape=jax.ShapeDtypeStruct((m, n), BF16),
        compiler_params=_cparams(2),
        name="ffn_swiglu",
    )(a, wg, wu)


def _mla_prep_kernel(m_ref, gq_ref, gkv_ref, bf_ref, cos_ref, sin_ref,
                     ql_ref, ckv32_ref, ckv16_ref, kr_ref, *, q_lora, kv_lora, n_forget):
    x = m_ref[...]
    ql_ref[...] = (_rms(x[:, :q_lora]) * gq_ref[...]).astype(BF16)
    ckv = _rms(x[:, q_lora:q_lora + kv_lora]) * gkv_ref[...]
    ckv32_ref[...] = ckv
    ckv16_ref[...] = ckv.astype(BF16)
    t = x[:, q_lora + kv_lora:]
    kr = _rope_tile(t, cos_ref[...], sin_ref[...])
    z = t + bf_ref[...]
    logf = jnp.minimum(z, 0.0) - _softplus_neg_abs(z)
    lane = lax.broadcasted_iota(jnp.int32, t.shape, 1)
    rope_w = LANES // 2
    kr_ref[...] = jnp.where(lane < rope_w, kr, jnp.where(lane < rope_w + n_forget, logf, 0.0))


def _mla_prep(mlat, gq, gkv, bf_tile, cos_t, sin_t, q_lora, kv_lora, n_forget):
    m = mlat.shape[0]
    s = cos_t.shape[0]
    tr = _tile(s, 256, SUBLANES)
    ns = s // tr
    rows = lambda w: pl.BlockSpec((tr, w), lambda i: (i, 0))
    shared = lambda w: pl.BlockSpec((1, w), lambda i: (0, 0))
    table = pl.BlockSpec((tr, LANES), lambda i: (i % ns, 0))
    return pl.pallas_call(
        functools.partial(_mla_prep_kernel, q_lora=q_lora, kv_lora=kv_lora, n_forget=n_forget),
        grid=(m // tr,),
        in_specs=[rows(mlat.shape[1]), shared(q_lora), shared(kv_lora), shared(LANES), table, table],
        out_specs=[rows(q_lora), rows(kv_lora), rows(kv_lora), rows(LANES)],
        out_shape=[
            jax.ShapeDtypeStruct((m, q_lora), BF16),
            jax.ShapeDtypeStruct((m, kv_lora), F32),
            jax.ShapeDtypeStruct((m, kv_lora), BF16),
            jax.ShapeDtypeStruct((m, LANES), F32),
        ],
        compiler_params=_cparams(1),
        name="mla_prep",
    )(mlat, gq.reshape(1, q_lora), gkv.reshape(1, kv_lora), bf_tile, cos_t, sin_t)


def _q_up_kernel(a_ref, w_ref, cos_ref, sin_ref, o_ref, *, n_heads, out_scale):
    acc = _dot(a_ref[...], w_ref[...]) * out_scale
    cos = cos_ref[...]
    sin = sin_ref[...]
    for h in range(n_heads):
        base = h * MLA_QK_PAD
        o_ref[:, base:base + MLA_NOPE] = acc[:, base:base + MLA_NOPE].astype(BF16)
        t = acc[:, base + MLA_NOPE:base + MLA_QK_PAD]
        o_ref[:, base + MLA_NOPE:base + MLA_QK_PAD] = _rope_tile(t, cos, sin).astype(BF16)


def _q_up(qlat, wq, cos_t, sin_t, n_heads, out_scale):
    m, k = qlat.shape
    n = wq.shape[1]
    s = cos_t.shape[0]
    tr = _tile(s, 512, SUBLANES)
    ns = s // tr
    table = pl.BlockSpec((tr, LANES), lambda i: (i % ns, 0))
    return pl.pallas_call(
        functools.partial(_q_up_kernel, n_heads=n_heads, out_scale=out_scale),
        grid=(m // tr,),
        in_specs=[pl.BlockSpec((tr, k), lambda i: (i, 0)), pl.BlockSpec((k, n), lambda i: (0, 0)), table, table],
        out_specs=pl.BlockSpec((tr, n), lambda i: (i, 0)),
        out_shape=jax.ShapeDtypeStruct((m, n), BF16),
        compiler_params=_cparams(1),
        name="mla_q_up",
    )(qlat, wq, cos_t, sin_t)


def _kv_up_kernel(a_ref, wk_ref, wv_ref, kr_ref, k_ref, v_ref, *, n_heads):
    a = a_ref[...].astype(BF16)
    kn = _dot(a, wk_ref[...])
    v_ref[...] = _dot(a, wv_ref[...]).astype(BF16)
    kr = kr_ref[...]
    lane = lax.broadcasted_iota(jnp.int32, kr.shape, 1)
    krm = jnp.where(lane < LANES // 2, kr, 0.0).astype(BF16)
    for h in range(n_heads):
        base = h * MLA_QK_PAD
        k_ref[:, base:base + MLA_NOPE] = kn[:, h * MLA_NOPE:(h + 1) * MLA_NOPE].astype(BF16)
        k_ref[:, base + MLA_NOPE:base + MLA_QK_PAD] = krm


def _kv_up(ckv, wk, wv, kr, n_heads):
    m, k = ckv.shape
    tm = _tile(m, 512, SUBLANES)
    rows = lambda w: pl.BlockSpec((tm, w), lambda i: (i, 0))
    whole = lambda a: pl.BlockSpec(a.shape, lambda i: (0, 0))
    return pl.pallas_call(
        functools.partial(_kv_up_kernel, n_heads=n_heads),
        grid=(m // tm,),
        in_specs=[rows(k), whole(wk), whole(wv), rows(LANES)],
        out_specs=[rows(n_heads * MLA_QK_PAD), rows(n_heads * MLA_V)],
        out_shape=[
            jax.ShapeDtypeStruct((m, n_heads * MLA_QK_PAD), BF16),
            jax.ShapeDtypeStruct((m, n_heads * MLA_V), BF16),
        ],
        compiler_params=_cparams(1),
        name="mla_kv_up",
    )(ckv, wk, wv, kr)


def _split3(x):
    x1 = x.astype(BF16)
    r1 = x - x1.astype(F32)
    x2 = r1.astype(BF16)
    x3 = (r1 - x2.astype(F32)).astype(BF16)
    return x1, x2, x3


def _cumsum_kernel(x_ref, o_ref, *rest, tc, bias_lanes):
    carry_ref = rest[-1]

    @pl.when(pl.program_id(1) == 0)
    def _():
        carry_ref[...] = jnp.zeros_like(carry_ref)

    r = lax.broadcasted_iota(jnp.int32, (tc, tc), 0)
    c = lax.broadcasted_iota(jnp.int32, (tc, tc), 1)
    tri = jnp.where(c <= r, 1.0, 0.0).astype(BF16)
    x1, x2, x3 = _split3(x_ref[0])
    y = (_dot(tri, x3) + _dot(tri, x2)) + _dot(tri, x1) + carry_ref[...]
    o_ref[0] = y
    carry_ref[...] = y[tc - 1:tc, :]
    if bias_lanes is not None:
        kb_ref = rest[0]
        lane0, n_heads = bias_lanes
        parts = [p.astype(F32) for p in _split3(y * (-LOG2E))]
        lane = lax.broadcasted_iota(jnp.int32, (tc, LANES), 1)
        for h in range(n_heads):
            col = [jnp.broadcast_to(p[:, lane0 + h:lane0 + h + 1], (tc, LANES)) for p in parts]
            blk = jnp.where(lane == 0, col[0], jnp.where(lane == 1, col[1], jnp.where(lane == 2, col[2], 0.0)))
            kb_ref[0, :, h * LANES:(h + 1) * LANES] = blk.astype(BF16)


def _cumsum_rows(x, bias_lanes=None):
    b, s, w = x.shape
    tc = _tile(s, 256, 2 * SUBLANES)
    blk = pl.BlockSpec((1, tc, w), lambda bi, j: (bi, j, 0))
    out_specs, out_shape = [blk], [jax.ShapeDtypeStruct((b, s, w), F32)]
    if bias_lanes is not None:
        wide = bias_lanes[1] * LANES
        out_specs.append(pl.BlockSpec((1, tc, wide), lambda bi, j: (bi, j, 0)))
        out_shape.append(jax.ShapeDtypeStruct((b, s, wide), BF16))
    outs = pl.pallas_call(
        functools.partial(_cumsum_kernel, tc=tc, bias_lanes=bias_lanes),
        grid=(b, s // tc),
        in_specs=[blk],
        out_specs=out_specs,
        out_shape=out_shape,
        scratch_shapes=[pltpu.VMEM((1, w), F32)],
        compiler_params=_cparams(2, sequential_axes=(1,)),
        name="cumsum_rows",
    )(x)
    return outs if bias_lanes is not None else outs[0]


def _rep_lanes(x, width):
    return x if width == LANES else jnp.concatenate([x] * (width // LANES), axis=1)


def _softmax_attn_kernel(*refs, t, group, dq, dv, chunk, has_bias):
    if has_bias:
        qt_ref, k_ref, kb_ref, vt_ref, o_ref, m_ref, l_ref, acc_ref, sa_ref, sb_ref = refs
    else:
        qt_ref, k_ref, vt_ref, o_ref, m_ref, l_ref, acc_ref, sa_ref, sb_ref = refs
    i = pl.program_id(2)
    m_ref[...] = jnp.full_like(m_ref, -jnp.inf)
    l_ref[...] = jnp.zeros_like(l_ref)
    acc_ref[...] = jnp.zeros_like(acc_ref)
    if has_bias:
        row = lax.broadcasted_iota(jnp.int32, (LANES, t), 0)
        ones_rows = jnp.where(row < N_BIAS_TERMS, 1.0, 0.0).astype(BF16)

    def score(g, ks):
        k = k_ref[0, pl.ds(ks, t), g * dq:(g + 1) * dq]
        qt = qt_ref[0, g * dq:(g + 1) * dq, :]
        if has_bias:
            k = jnp.concatenate([k, kb_ref[0, pl.ds(ks, t), g * LANES:(g + 1) * LANES]], axis=1)
            qt = jnp.concatenate([qt, ones_rows], axis=0)
        return _dot(k, qt)

    heads = range(group)

    def scores_into(j, s_ref):
        ks = pl.multiple_of(j * t, t)
        for g in heads:
            s_ref[g] = score(g, ks)

    def fold(j, s_ref, masked):
        ks = pl.multiple_of(j * t, t)
        probs, alphas = [], []
        for g in heads:
            s = s_ref[g]
            if masked:
                kpos = lax.broadcasted_iota(jnp.int32, s.shape, 0)
                qpos = lax.broadcasted_iota(jnp.int32, s.shape, 1)
                visible = (kpos // chunk) <= (qpos // chunk) if chunk else kpos <= qpos
                s = jnp.where(visible, s, -jnp.inf)
            m_prev = m_ref[g]
            m_new = jnp.maximum(m_prev, jnp.max(s, axis=0, keepdims=True))
            alpha = jnp.exp2(m_prev - m_new)
            p = jnp.exp2(s - m_new)
            l_ref[g] = alpha * l_ref[g] + jnp.sum(p, axis=0, keepdims=True)
            m_ref[g] = m_new
            probs.append(p.astype(BF16))
            alphas.append(alpha)
        for g in heads:
            pv = _dot(vt_ref[0, g * dv:(g + 1) * dv, pl.ds(ks, t)], probs[g])
            acc_ref[g] = alphas[g] * acc_ref[g] + pv

    scores_into(0, sa_ref)

    def body(jj, carry):
        j = 2 * jj
        scores_into(j + 1, sb_ref)
        fold(j, sa_ref, False)
        scores_into(j + 2, sa_ref)
        fold(j + 1, sb_ref, False)
        return carry

    lax.fori_loop(0, i // 2, body, 0)

    @pl.when(i % 2 == 0)
    def _():
        fold(i, sa_ref, True)

    @pl.when(i % 2 == 1)
    def _():
        scores_into(i, sb_ref)
        fold(i - 1, sa_ref, False)
        fold(i, sb_ref, True)

    for g in range(group):
        o_ref[0, :, g * dv:(g + 1) * dv] = (acc_ref[g] / l_ref[g]).T


def _head_group(n_heads):
    return ATTN_HEAD_GROUP if n_heads % ATTN_HEAD_GROUP == 0 else 1


def _softmax_attn(qt, k, vt, n_heads, dq, dv, chunk, name, kb=None):
    b, s, _ = k.shape
    t = _tile(s, ATTN_TILE, LANES)
    grp = _head_group(n_heads)
    stat = pltpu.VMEM((grp, 1, t), F32)
    keys = lambda w: pl.BlockSpec((1, s, grp * w), lambda bi, h, i: (bi, 0, h))
    in_specs = [pl.BlockSpec((1, grp * dq, t), lambda bi, h, i: (bi, h, i)), keys(dq)]
    args = [qt, k]
    if kb is not None:
        in_specs.append(keys(LANES))
        args.append(kb)
    in_specs.append(pl.BlockSpec((1, grp * dv, s), lambda bi, h, i: (bi, h, 0)))
    args.append(vt)
    return pl.pallas_call(
        functools.partial(_softmax_attn_kernel, t=t, group=grp, dq=dq, dv=dv, chunk=chunk, has_bias=kb is not None),
        grid=(b, n_heads // grp, s // t),
        in_specs=in_specs,
        out_specs=pl.BlockSpec((1, t, grp * dv), lambda bi, h, i: (bi, i, h)),
        out_shape=jax.ShapeDtypeStruct((b, s, n_heads * dv), F32),
        scratch_shapes=[stat, stat, pltpu.VMEM((grp, dv, t), F32),
                        pltpu.VMEM((grp, t, t), F32), pltpu.VMEM((grp, t, t), F32)],
        compiler_params=_cparams(3),
        name=name,
    )(*args)


def _softmax_attn_cached_kernel(*refs, has_bias, chunk, past):
    if has_bias:
        q_ref, kc_ref, vc_ref, kn_ref, vn_ref, fkc_ref, fkn_ref, o_ref = refs
    else:
        q_ref, kc_ref, vc_ref, kn_ref, vn_ref, o_ref = refs
    q = q_ref[...]
    s_c = _dot_nt(q, kc_ref[...].astype(BF16))
    s_n = _dot_nt(q, kn_ref[...])
    if has_bias:
        s_c = s_c - fkc_ref[...] * LOG2E
        s_n = s_n - fkn_ref[...] * LOG2E
    r = lax.broadcasted_iota(jnp.int32, s_n.shape, 0) + past
    c = lax.broadcasted_iota(jnp.int32, s_n.shape, 1) + past
    visible = (c // chunk) <= (r // chunk) if chunk else c <= r
    s_n = jnp.where(visible, s_n, -jnp.inf)
    m = jnp.maximum(jnp.max(s_c, axis=-1, keepdims=True), jnp.max(s_n, axis=-1, keepdims=True))
    p_c = jnp.exp2(s_c - m)
    p_n = jnp.exp2(s_n - m)
    l = jnp.sum(p_c, axis=-1, keepdims=True) + jnp.sum(p_n, axis=-1, keepdims=True)
    acc = _dot(p_c.astype(BF16), vc_ref[...].astype(BF16)) + _dot(p_n.astype(BF16), vn_ref[...])
    o_ref[...] = acc / l


def _cache_spec(cache, layer, d):
    if layer is None:
        return pl.BlockSpec((None, cache.shape[1], d), lambda bi, h: (bi, 0, h))
    return pl.BlockSpec((None, None, cache.shape[2], d), lambda bi, h: (layer, bi, 0, h))


def _softmax_attn_cached(q, kc, vc, kn, vn, n_heads, dq, dv, chunk, f_row_c=None, f_row_n=None, layer=None):
    b, sn, _ = q.shape
    past = kc.shape[-2]
    has_bias = f_row_c is not None
    new = lambda d: pl.BlockSpec((None, sn, d), lambda bi, h: (bi, 0, h))
    in_specs = [new(dq), _cache_spec(kc, layer, dq), _cache_spec(vc, layer, dv), new(dq), new(dv)]
    args = [q, kc, vc, kn, vn]
    if has_bias:
        in_specs += [
            pl.BlockSpec((None, None, 1, past), lambda bi, h: (bi, h, 0, 0)),
            pl.BlockSpec((None, None, 1, sn), lambda bi, h: (bi, h, 0, 0)),
        ]
        args += [f_row_c, f_row_n]
    return pl.pallas_call(
        functools.partial(_softmax_attn_cached_kernel, has_bias=has_bias, chunk=chunk, past=past),
        grid=(b, n_heads),
        in_specs=in_specs,
        out_specs=new(dv),
        out_shape=jax.ShapeDtypeStruct((b, sn, n_heads * dv), F32),
        compiler_params=_cparams(2),
        name="fox_attn_cached" if has_bias else "mla_attn_cached",
    )(*args)


def _cache_block(layer, tp, n_heads, d, order):
    return pl.BlockSpec((None, None, tp, n_heads, d), lambda bi, j: (layer, bi, order(j), 0, 0))


def _fox_cached_kernel(q_ref, kc_ref, vc_ref, kn_ref, vn_ref, fkc_ref, fkn_ref, o_ref, m_ref, l_ref, acc_ref,
                       *, n_heads, d):
    j = pl.program_id(1)

    def update(h, s, v):
        m_prev = m_ref[h]
        m_new = jnp.maximum(m_prev, jnp.max(s, axis=-1, keepdims=True))
        alpha = jnp.exp2(m_prev - m_new)
        p = jnp.exp2(s - m_new)
        l_ref[h] = alpha * l_ref[h] + jnp.sum(p, axis=-1, keepdims=True)
        acc_ref[h] = alpha * acc_ref[h] + _dot(p.astype(BF16), v)
        m_ref[h] = m_new

    @pl.when(j == 0)
    def _():
        m_ref[...] = jnp.full_like(m_ref, -jnp.inf)
        l_ref[...] = jnp.zeros_like(l_ref)
        acc_ref[...] = jnp.zeros_like(acc_ref)
        for h in range(n_heads):
            cols = slice(h * d, (h + 1) * d)
            s = _dot_nt(q_ref[:, cols], kn_ref[:, cols]) - fkn_ref[h] * LOG2E
            r = lax.broadcasted_iota(jnp.int32, s.shape, 0)
            c = lax.broadcasted_iota(jnp.int32, s.shape, 1)
            update(h, jnp.where(c <= r, s, -jnp.inf), vn_ref[:, cols])

    for h in range(n_heads):
        s = _dot_nt(q_ref[:, h * d:(h + 1) * d], kc_ref[:, h, :].astype(BF16)) - fkc_ref[h] * LOG2E
        update(h, s, vc_ref[:, h, :].astype(BF16))

    @pl.when(j == pl.num_programs(1) - 1)
    def _():
        for h in range(n_heads):
            o_ref[:, h * d:(h + 1) * d] = acc_ref[h] / l_ref[h]


def _fox_attn_cached(q, kc, vc, kn, vn, f_row_c, f_row_n, layer):
    b, sn, _ = q.shape
    _, _, past, n_heads, d = kc.shape
    tp = _tile(past, CACHE_BLOCK, LANES)
    new = pl.BlockSpec((None, sn, n_heads * d), lambda bi, j: (bi, 0, 0))
    cache = _cache_block(layer, tp, n_heads, d, lambda j: j)
    stat = pltpu.VMEM((n_heads, sn, 1), F32)
    return pl.pallas_call(
        functools.partial(_fox_cached_kernel, n_heads=n_heads, d=d),
        grid=(b, past // tp),
        in_specs=[new, cache, cache, new, new,
                  pl.BlockSpec((None, n_heads, 1, tp), lambda bi, j: (bi, 0, 0, j)),
                  pl.BlockSpec((None, n_heads, 1, sn), lambda bi, j: (bi, 0, 0, 0))],
        out_specs=new,
        out_shape=jax.ShapeDtypeStruct((b, sn, n_heads * d), F32),
        scratch_shapes=[stat, stat, pltpu.VMEM((n_heads, sn, d), F32)],
        compiler_params=_cparams(2, sequential_axes=(1,)),
        name="fox_attn_cached",
    )(q, kc, vc, kn, vn, f_row_c, f_row_n)


def _suffix_matrix(t):
    r = lax.broadcasted_iota(jnp.int32, (t, t), 0)
    c = lax.broadcasted_iota(jnp.int32, (t, t), 1)
    return jnp.where(r >= c, 1.0, 0.0).astype(BF16)


def _sb_fold(q, k, v, u, c, strict):
    z = _dot_nt(q, k)
    neg_abs = lax.bitcast_convert_type(lax.bitcast_convert_type(z, jnp.uint32) | jnp.uint32(0x80000000), F32)
    sp = jnp.maximum(z, 0.0) + jnp.log(1.0 + jnp.exp2(neg_abs)) * LOG2E
    if strict is not None:
        sp = jnp.where(strict, sp, 0.0)
    hi = lax.bitcast_convert_type(lax.bitcast_convert_type(sp, jnp.uint32) & jnp.uint32(0xFFFF0000), F32)
    lo = (sp - hi).astype(BF16)
    suffix = _dot(lo, u) + _dot(hi.astype(BF16), u)
    width = z.shape[1]
    c_wide = c[:, :width] if width < LANES else _rep_lanes(c, width)
    a = jnp.exp2(z - (suffix + c_wide))
    if strict is not None:
        a = jnp.where(strict, a, 0.0)
    return _dot(a.astype(BF16), v), c + jnp.sum(sp, axis=-1, keepdims=True)


def _sb_attn_kernel(qt_ref, k_ref, vt_ref, o_ref, c_ref, acc_ref, *bufs, t, tsub, group, d):
    i = pl.program_id(2)
    buf_a, buf_b = bufs[:3], bufs[3:]
    r = lax.broadcasted_iota(jnp.int32, (tsub, tsub), 0)
    col = lax.broadcasted_iota(jnp.int32, (tsub, tsub), 1)
    ut = jnp.where(col >= r, 1.0, 0.0).astype(BF16)
    c_ref[...] = jnp.zeros_like(c_ref)
    acc_ref[...] = jnp.zeros_like(acc_ref)
    n_sub = t // tsub
    chains = [(g, sub) for g in range(group) for sub in reversed(range(n_sub))]

    def strict_mask(sub):
        kpos = lax.broadcasted_iota(jnp.int32, (tsub, t), 0) + sub * tsub
        qpos = lax.broadcasted_iota(jnp.int32, (tsub, t), 1)
        return kpos < qpos

    def scores(j, buf, masked):
        z_ref, hi_ref, lo_ref = buf
        z = {}
        for g, sub in chains:
            start = pl.multiple_of(j * t + sub * tsub, tsub)
            z[g, sub] = _dot(k_ref[0, pl.ds(start, tsub), g * d:(g + 1) * d], qt_ref[0, g * d:(g + 1) * d, :])
        for g, sub in chains:
            zc = z[g, sub]
            neg_abs = lax.bitcast_convert_type(lax.bitcast_convert_type(zc, jnp.uint32) | jnp.uint32(0x80000000), F32)
            sp = jnp.maximum(zc, 0.0) + jnp.log(1.0 + jnp.exp2(neg_abs)) * LOG2E
            if masked:
                sp = jnp.where(strict_mask(sub), sp, 0.0)
            hi = lax.bitcast_convert_type(lax.bitcast_convert_type(sp, jnp.uint32) & jnp.uint32(0xFFFF0000), F32)
            z_ref[g, sub] = zc
            hi_ref[g, sub] = hi.astype(BF16)
            lo_ref[g, sub] = (sp - hi).astype(BF16)

    def fold(j, buf, masked):
        z_ref, hi_ref, lo_ref = buf
        suffix = {(g, sub): _dot(ut, lo_ref[g, sub]) + _dot(ut, hi_ref[g, sub]) for g, sub in chains}
        c = {g: c_ref[g] for g in range(group)}
        acc = {g: acc_ref[g] for g in range(group)}
        for g, sub in chains:
            start = pl.multiple_of(j * t + sub * tsub, tsub)
            a = jnp.exp2(z_ref[g, sub] - (suffix[g, sub] + c[g]))
            if masked:
                a = jnp.where(strict_mask(sub), a, 0.0)
            acc[g] = acc[g] + _dot(vt_ref[0, g * d:(g + 1) * d, pl.ds(start, tsub)], a.astype(BF16))
            c[g] = c[g] + suffix[g, sub][0:1, :]
        for g in range(group):
            c_ref[g] = c[g]
            acc_ref[g] = acc[g]

    scores(i, buf_a, True)
    fold(i, buf_a, True)
    block = lambda n: jnp.maximum(i - 1 - n, 0)
    scores(block(0), buf_a, False)

    def body(jj, carry):
        n = 2 * jj
        scores(block(n + 1), buf_b, False)
        fold(block(n), buf_a, False)
        scores(block(n + 2), buf_a, False)
        fold(block(n + 1), buf_b, False)
        return carry

    lax.fori_loop(0, i // 2, body, 0)

    @pl.when(i % 2 == 1)
    def _():
        fold(0, buf_a, False)

    for g in range(group):
        o_ref[0, :, g * d:(g + 1) * d] = acc_ref[g].T


def _sb_attn(qt, k, vt, n_heads):
    b, s, _ = k.shape
    d = HEAD_DIM
    t = _tile(s, ATTN_TILE, LANES)
    tsub = _tile(t, SB_SUB_TILE, LANES)
    grp = _head_group(n_heads)
    return pl.pallas_call(
        functools.partial(_sb_attn_kernel, t=t, tsub=tsub, group=grp, d=d),
        grid=(b, n_heads // grp, s // t),
        in_specs=[
            pl.BlockSpec((1, grp * d, t), lambda bi, h, i: (bi, h, i)),
            pl.BlockSpec((1, s, grp * d), lambda bi, h, i: (bi, 0, h)),
            pl.BlockSpec((1, grp * d, s), lambda bi, h, i: (bi, h, 0)),
        ],
        out_specs=pl.BlockSpec((1, t, grp * d), lambda bi, h, i: (bi, i, h)),
        out_shape=jax.ShapeDtypeStruct((b, s, n_heads * d), F32),
        scratch_shapes=[pltpu.VMEM((grp, 1, t), F32), pltpu.VMEM((grp, d, t), F32)]
        + [pltpu.VMEM((grp, t // tsub, tsub, t), dt) for dt in (F32, BF16, BF16)] * 2,
        compiler_params=_cparams(3),
        name="sb_attn",
    )(qt, k, vt)


def _sb_cached_kernel(q_ref, kc_ref, vc_ref, kn_ref, vn_ref, o_ref, c_ref, acc_ref, *, n_heads, d, tk):
    j = pl.program_id(1)
    sn = q_ref.shape[0]

    @pl.when(j == 0)
    def _():
        r = lax.broadcasted_iota(jnp.int32, (sn, sn), 0)
        col = lax.broadcasted_iota(jnp.int32, (sn, sn), 1)
        u_new = _suffix_matrix(sn)
        for h in range(n_heads):
            cols = slice(h * d, (h + 1) * d)
            acc_ref[h], c_ref[h] = _sb_fold(q_ref[:, cols], kn_ref[:, cols], vn_ref[:, cols], u_new,
                                            jnp.zeros((sn, LANES), F32), col < r)

    u = _suffix_matrix(tk)
    for h in range(n_heads):
        q = q_ref[:, h * d:(h + 1) * d]
        c = c_ref[h]
        acc = acc_ref[h]
        for blk in reversed(range(kc_ref.shape[0] // tk)):
            rows = slice(blk * tk, (blk + 1) * tk)
            pv, c = _sb_fold(q, kc_ref[rows, h, :].astype(BF16), vc_ref[rows, h, :].astype(BF16), u, c, None)
            acc = acc + pv
        c_ref[h] = c
        acc_ref[h] = acc

    @pl.when(j == pl.num_programs(1) - 1)
    def _():
        for h in range(n_heads):
            o_ref[:, h * d:(h + 1) * d] = acc_ref[h]


def _sb_attn_cached(q, kc, vc, kn, vn, layer):
    b, sn, _ = q.shape
    _, _, past, n_heads, d = kc.shape
    tp = _tile(past, CACHE_BLOCK, LANES)
    tk = _tile(tp, SB_SUB_TILE, LANES)
    n_blocks = past // tp
    new = pl.BlockSpec((None, sn, n_heads * d), lambda bi, j: (bi, 0, 0))
    cache = _cache_block(layer, tp, n_heads, d, lambda j: n_blocks - 1 - j)
    return pl.pallas_call(
        functools.partial(_sb_cached_kernel, n_heads=n_heads, d=d, tk=tk),
        grid=(b, n_blocks),
        in_specs=[new, cache, cache, new, new],
        out_specs=new,
        out_shape=jax.ShapeDtypeStruct((b, sn, n_heads * d), F32),
        scratch_shapes=[pltpu.VMEM((n_heads, sn, LANES), F32), pltpu.VMEM((n_heads, sn, d), F32)],
        compiler_params=_cparams(2, sequential_axes=(1,)),
        name="sb_attn_cached",
    )(q, kc, vc, kn, vn)


def _group_norm_kernel(of_ref, os_ref, om_ref, g_ref, h_ref):
    off = 0
    for ref in (of_ref, os_ref, om_ref):
        w = ref.shape[-1]
        h_ref[0, :, off:off + w] = (_rms(ref[0]) * g_ref[:, off:off + w]).astype(BF16)
        off += w


def _group_norm(o_fox, o_sb, o_mla, g):
    b, s, _ = o_fox.shape
    d = g.shape[-1]
    tr = _tile(s, 256, SUBLANES)
    rows = lambda a: pl.BlockSpec((1, tr, a.shape[-1]), lambda bi, i: (bi, i, 0))
    return pl.pallas_call(
        _group_norm_kernel,
        grid=(b, s // tr),
        in_specs=[rows(o_fox), rows(o_sb), rows(o_mla), pl.BlockSpec((1, d), lambda bi, i: (0, 0))],
        out_specs=pl.BlockSpec((1, tr, d), lambda bi, i: (bi, i, 0)),
        out_shape=jax.ShapeDtypeStruct((b, s, d), BF16),
        compiler_params=_cparams(2),
        name="group_norm",
    )(o_fox, o_sb, o_mla, g.reshape(1, d))


def _rope_tables(pos, rope_dim):
    half = rope_dim // 2
    inv = ROPE_THETA ** (-jnp.arange(half, dtype=F32) / half)
    ang = pos.astype(F32)[:, None] * inv[None, :]
    cos, sin = jnp.cos(ang), jnp.sin(ang)
    n = pos.shape[0]
    pad = LANES - rope_dim
    cos_t = jnp.concatenate([cos, cos, jnp.ones((n, pad), F32)], axis=-1)
    sin_t = jnp.concatenate([-sin, sin, jnp.zeros((n, pad), F32)], axis=-1)
    return cos_t, sin_t


def _prep_layer_weights(l, dims, w_in, b_forget, w_q_up, w_kv_up):
    hf, hs, hm, q_lora, kv_lora, rope = dims
    wf, ws = hf * HEAD_DIM, hs * HEAD_DIM
    d = w_in.shape[1]
    w = w_in[l]
    widths = (wf, wf, wf, hf, ws, ws, ws, q_lora, kv_lora, rope)
    offs = [0]
    for x in widths:
        offs.append(offs[-1] + x)
    col = lambda i: w[:, offs[i]:offs[i + 1]]
    lw = {"fox_cols": (offs[0], offs[1], offs[2])}
    for name, i in (("sq", 4), ("sk", 5), ("sv", 6)):
        lw[name] = col(i).astype(BF16)
    pad = jnp.zeros((d, LANES - rope - hf), F32)
    lw["mla_in"] = jnp.concatenate([col(7), col(8), col(9), col(3), pad], axis=1).astype(BF16)
    lw["bf"] = jnp.zeros((1, LANES), F32).at[0, rope:rope + hf].set(b_forget[l])
    wq = w_q_up[l].reshape(q_lora, hm, MLA_NOPE + rope)
    wq = jnp.pad(wq, ((0, 0), (0, 0), (0, MLA_QK_PAD - MLA_NOPE - rope)))
    lw["wq"] = wq.reshape(q_lora, hm * MLA_QK_PAD).astype(BF16)
    wkv = w_kv_up[l].reshape(kv_lora, hm, MLA_NOPE + MLA_V)
    lw["wk"] = wkv[:, :, :MLA_NOPE].reshape(kv_lora, hm * MLA_NOPE).astype(BF16)
    lw["wv"] = wkv[:, :, MLA_NOPE:].reshape(kv_lora, hm * MLA_V).astype(BF16)
    return lw


def _ffn(h, gw, l, f):
    b, s, d = h.shape
    a = _swiglu(h.reshape(b * s, d), gw["wg"], gw["wu"], (l, f))
    (y,) = _mm(a, gw["wd"], (F32,), "ffn_down", w_index=(l, f), tm_pref=512, tn_pref=512)
    return y.reshape(b, s, d)


def _token_mix(h, l, lw, gw, dims, tables, cache):
    hf, hs, hm, q_lora, kv_lora, rope = dims
    b, s, d = h.shape
    h2 = h.reshape(b * s, d)
    cos_t, sin_t = tables
    three = lambda a: a.reshape(b, s, -1)

    qk_scale = HEAD_DIM ** -0.5 * LOG2E
    mla_scale = (MLA_NOPE + rope) ** -0.5 * LOG2E
    fq_col, fk_col, fv_col = lw["fox_cols"]
    wf = hf * HEAD_DIM
    (fq,) = _mm(h2, gw["w_fox"], (BF16,), "in_fox_q", w_index=(l,), col0=fq_col, n=wf, out_scale=qk_scale)
    fk32, fk16 = _mm_state(h2, gw["w_fox"], (l,), fk_col, hf, "in_fox_k")
    fv32, fv16 = _mm_state(h2, gw["w_fox"], (l,), fv_col, hf, "in_fox_v")
    (sq,) = _mm(h2, lw["sq"], (BF16,), "in_sb_q", out_scale=qk_scale)
    sk32, sk16 = _mm_state(h2, lw["sk"], (), 0, hs, "in_sb_k")
    sv32, sv16 = _mm_state(h2, lw["sv"], (), 0, hs, "in_sb_v")
    (mlat,) = _mm(h2, lw["mla_in"], (F32,), "in_mla", tm_pref=512, tn_pref=lw["mla_in"].shape[1])

    qlat, ckv32, ckv16, krlf = _mla_prep(mlat, lw["gq"], lw["gkv"], lw["bf"], cos_t, sin_t, q_lora, kv_lora, hf)
    q_mla = _q_up(qlat, lw["wq"], cos_t, sin_t, hm, mla_scale)
    k_mla, v_mla = _kv_up(ckv16, lw["wk"], lw["wv"], krlf, hm)

    krlf3 = three(krlf)
    krope = krlf3[:, :, :rope]
    logf = krlf3[:, :, rope:rope + hf]

    if cache is None:
        transposed = lambda a: jnp.swapaxes(three(a), 1, 2)
        _, k_bias = _cumsum_rows(krlf3, bias_lanes=(rope, hf))
        o_fox = _softmax_attn(transposed(fq), three(fk16), transposed(fv16), hf, HEAD_DIM, HEAD_DIM, 0,
                              "fox_attn", kb=k_bias)
        o_sb = _sb_attn(transposed(sq), three(sk16), transposed(sv16), hs)
        o_mla = _softmax_attn(transposed(q_mla), three(k_mla), transposed(v_mla), hm, MLA_QK_PAD, MLA_V, CHUNK, "mla_attn")
    else:
        c_fk, c_fv, c_logf, c_sk, c_sv, c_ckv, c_krope = cache
        past = c_fk.shape[2]
        flat = lambda a: a.reshape(b, past, -1)
        logf_all = jnp.concatenate([c_logf[l].astype(F32), logf], axis=1)
        logf_all = jnp.pad(logf_all, ((0, 0), (0, 0), (0, LANES - hf)))
        f_all = _cumsum_rows(logf_all)[:, :, :hf]
        f_bh = jnp.swapaxes(f_all, 1, 2)
        o_fox = _fox_attn_cached(three(fq), c_fk, c_fv, three(fk16), three(fv16),
                                 f_bh[:, :, None, :past], f_bh[:, :, None, past:], l)
        o_sb = _sb_attn_cached(three(sq), c_sk, c_sv, three(sk16), three(sv16), l)
        c_kr = jnp.pad(c_krope[l].reshape(b * past, rope), ((0, 0), (0, LANES - rope)))
        kc_mla, vc_mla = _kv_up(c_ckv[l].reshape(b * past, kv_lora), lw["wk"], lw["wv"], c_kr, hm)
        o_mla = _softmax_attn_cached(
            three(q_mla), flat(kc_mla), flat(vc_mla), three(k_mla), three(v_mla), hm, MLA_QK_PAD, MLA_V, CHUNK)

    hmix = _group_norm(o_fox, o_sb, o_mla, lw["g_mix"])
    (mix,) = _mm(hmix.reshape(b * s, d), gw["w_out"], (F32,), "mix_out", w_index=(l,))
    per_head = lambda a: a.reshape((b, s) + a.shape[1:])
    new_state = (per_head(fk32), per_head(fv32), logf, per_head(sk32), per_head(sv32),
                 ckv32.reshape(b, s, kv_lora), krope)
    return mix.reshape(b, s, d), new_state


def kernel(x_prompt, x_sample, c_prompt, c_sample, cache_fox_k, cache_fox_v, cache_fox_logf, cache_sb_k, cache_sb_v, cache_mla_ckv, cache_mla_krope, w_ada, b_ada, g_norm, w_in, b_forget, g_q_lat, w_q_up, g_kv_lat, w_kv_up, g_mix_out, w_out, w_ffn_gate, w_ffn_up, w_ffn_down, g_final):
    depth = w_ada.shape[0]
    d = x_prompt.shape[-1]
    nbp, nbs = x_prompt.shape[0], x_sample.shape[0]
    past = cache_fox_k.shape[2]
    hf, hs = cache_fox_k.shape[3], cache_sb_k.shape[3]
    kv_lora, rope, q_lora = cache_mla_ckv.shape[-1], cache_mla_krope.shape[-1], g_q_lat.shape[-1]
    hm = w_q_up.shape[-1] // (MLA_NOPE + rope)
    dims = (hf, hs, hm, q_lora, kv_lora, rope)

    c_all = jnp.concatenate([c_prompt, c_sample], axis=0)
    nb = nbp + nbs
    nb_pad = -(-nb // SUBLANES) * SUBLANES
    c_all = jnp.pad(c_all, ((0, nb_pad - nb), (0, 0)))
    mods = _ada(c_all, w_ada, b_ada).reshape(depth, nb_pad, N_MOD, 1, d)
    groups = {"p": (0, nbp), "s": (nbp, nb)}
    mod = lambda l, grp, i: mods[l, groups[grp][0]:groups[grp][1], i]

    tables = {
        "p": _rope_tables(jnp.arange(x_prompt.shape[1]), rope),
        "s": _rope_tables(past + jnp.arange(x_sample.shape[1]), rope),
    }
    caches = (cache_fox_k, cache_fox_v, cache_fox_logf, cache_sb_k, cache_sb_v, cache_mla_ckv, cache_mla_krope)

    gw = {"w_fox": w_in[:, :, :3 * hf * HEAD_DIM].astype(BF16), "w_out": w_out.astype(BF16),
          "wg": w_ffn_gate.astype(BF16), "wu": w_ffn_up.astype(BF16), "wd": w_ffn_down.astype(BF16)}

    xs = {"p": x_prompt, "s": x_sample}
    pending = {"p": None, "s": None}
    states = {"p": [], "s": []}
    for l in range(depth):
        lw = _prep_layer_weights(l, dims, w_in, b_forget, w_q_up, w_kv_up)
        lw["gq"], lw["gkv"], lw["g_mix"] = g_q_lat[l], g_kv_lat[l], g_mix_out[l]
        for grp in ("p", "s"):
            m = lambda i: mod(l, grp, i)
            x, h = _prenorm(xs[grp], g_norm[l, 0], m(1), m(0), res=pending[grp])
            y = _ffn(h, gw, l, 0)
            x, h = _prenorm(x, g_norm[l, 1], m(4), m(3), res=(y, m(2), 0.5))
            cache = None if grp == "p" else caches
            mix, st = _token_mix(h, l, lw, gw, dims, tables[grp], cache)
            states[grp].append(st)
            x, h = _prenorm(x, g_norm[l, 2], m(7), m(6), res=(mix, m(5), 1.0))
            y = _ffn(h, gw, l, 1)
            xs[grp] = x
            pending[grp] = (y, m(8), 0.5)

    y_prompt = _prenorm(xs["p"], g_final, res=pending["p"], final=True)
    y_sample = _prenorm(xs["s"], g_final, res=pending["s"], final=True)

    stack = lambda grp, i: jnp.stack([st[i] for st in states[grp]])
    return (y_prompt, y_sample) + tuple(stack("p", i) for i in range(7)) + tuple(stack("s", i) for i in range(7))
```

```python
import functools

import jax
import jax.numpy as jnp
from jax import lax
from jax.experimental import pallas as pl
from jax.experimental.pallas import tpu as pltpu

F32 = jnp.float32
BF16 = jnp.bfloat16

HEAD_DIM = 128
MLA_NOPE = 128
MLA_V = 128
CHUNK = 64
N_MOD = 9
EPS = 1e-6
ROPE_THETA = 10000.0

LANES = 128
SUBLANES = 8
MLA_QK_PAD = 256
VMEM_LIMIT = 56 * 1024 * 1024
LOG2E = 1.4426950408889634

ATTN_TILE = 512
ATTN_HEAD_GROUP = 2
N_BIAS_TERMS = 3
CACHE_BLOCK = 1024
SB_SUB_TILE = 256
FUSED_RESIDUAL_MIN_ROWS = 256


def _cparams(n_axes, sequential_axes=()):
    sem = tuple("arbitrary" if a in sequential_axes else "parallel" for a in range(n_axes))
    return pltpu.CompilerParams(dimension_semantics=sem, vmem_limit_bytes=VMEM_LIMIT)


def _tile(n, pref, align):
    if n <= pref:
        return n
    for t in range(pref - pref % align, 0, -align):
        if n % t == 0:
            return t
    return n


def _dot(a, b):
    return jnp.dot(a, b, preferred_element_type=F32)


def _dot_nt(a, b):
    return lax.dot_general(a, b, (((1,), (1,)), ((), ())), preferred_element_type=F32)


def _softplus_neg_abs(z):
    return jnp.log1p(jnp.exp(-jnp.abs(z)))


def _rms(x):
    return x * lax.rsqrt(jnp.mean(x * x, axis=-1, keepdims=True) + EPS)


def _rope_tile(t, cos, sin):
    lane = lax.broadcasted_iota(jnp.int32, t.shape, 1)
    half = LANES // 4
    partner = jnp.where(lane < half, pltpu.roll(t, LANES - half, 1), pltpu.roll(t, half, 1))
    return t * cos + partner * sin


def _ada_kernel(c_ref, w_ref, b_ref, o_ref):
    c = c_ref[...]
    a = (c / (1.0 + jnp.exp(-c))).astype(BF16)
    o_ref[0] = _dot(a, w_ref[0].astype(BF16)) + b_ref[0]


def _ada(c_all, w_ada, b_ada):
    depth, d, n = w_ada.shape
    nb = c_all.shape[0]
    tn = _tile(n, 512, LANES)
    return pl.pallas_call(
        _ada_kernel,
        grid=(depth, n // tn),
        in_specs=[
            pl.BlockSpec((nb, d), lambda l, j: (0, 0)),
            pl.BlockSpec((1, d, tn), lambda l, j: (l, 0, j)),
            pl.BlockSpec((1, 1, tn), lambda l, j: (l, 0, j)),
        ],
        out_specs=pl.BlockSpec((1, nb, tn), lambda l, j: (l, 0, j)),
        out_shape=jax.ShapeDtypeStruct((depth, nb, n), F32),
        compiler_params=_cparams(2),
        name="ada",
    )(c_all, w_ada, b_ada.reshape(depth, 1, n))


def _prenorm_kernel(*refs, coef, has_res, final):
    refs = list(refs)
    x_ref = refs.pop(0)
    x = x_ref[0]
    if has_res:
        y_ref = refs.pop(0)
        gate_ref = refs.pop(0)
        x = x + (coef * gate_ref[0]) * y_ref[0]
    g_ref = refs.pop(0)
    if final:
        (o_ref,) = refs
        o_ref[0] = _rms(x) * g_ref[...]
        return
    sc_ref = refs.pop(0)
    sh_ref = refs.pop(0)
    if has_res:
        xo_ref = refs.pop(0)
        xo_ref[0] = x
    (h_ref,) = refs
    h = (_rms(x) * g_ref[...]) * (1.0 + sc_ref[0]) + sh_ref[0]
    h_ref[0] = h.astype(BF16)


def _prenorm(x, g, scale=None, shift=None, res=None, final=False):
    b, s, d = x.shape
    tr = _tile(s, 256, SUBLANES)
    row = pl.BlockSpec((1, tr, d), lambda bi, i: (bi, i, 0))
    per_b = pl.BlockSpec((1, 1, d), lambda bi, i: (bi, 0, 0))
    shared = pl.BlockSpec((1, d), lambda bi, i: (0, 0))
    args, specs = [x], [row]
    coef = 0.0
    if res is not None:
        y, gate, coef = res
        args += [y, gate]
        specs += [row, per_b]
    args.append(g.reshape(1, d))
    specs.append(shared)
    out_shape, out_specs = [], []
    if final:
        out_shape.append(jax.ShapeDtypeStruct((b, s, d), F32))
        out_specs.append(row)
    else:
        args += [scale, shift]
        specs += [per_b, per_b]
        if res is not None:
            out_shape.append(jax.ShapeDtypeStruct((b, s, d), F32))
            out_specs.append(row)
        out_shape.append(jax.ShapeDtypeStruct((b, s, d), BF16))
        out_specs.append(row)
    outs = pl.pallas_call(
        functools.partial(_prenorm_kernel, coef=coef, has_res=res is not None, final=final),
        grid=(b, s // tr),
        in_specs=specs,
        out_specs=out_specs,
        out_shape=out_shape,
        compiler_params=_cparams(2),
        name="final_norm" if final else "prenorm",
    )(*args)
    if final:
        return outs[0]
    if res is None:
        return x, outs[0]
    return outs[0], outs[1]


def _weight_spec(w, w_index, k, tn, col_block0=0):
    lead = tuple(w_index)
    return pl.BlockSpec((None,) * len(lead) + (k, tn), lambda i, j: lead + (0, col_block0 + j))


def _mm_kernel(a_ref, w_ref, *o_refs, out_scale):
    acc = _dot(a_ref[...].astype(BF16), w_ref[...])
    if out_scale != 1.0:
        acc = acc * out_scale
    for o_ref in o_refs:
        o_ref[...] = acc.astype(o_ref.dtype)


def _mm(a, w, out_dtypes, name, w_index=(), col0=0, n=None, tm_pref=1024, tn_pref=512, out_scale=1.0):
    m, k = a.shape
    n = w.shape[-1] if n is None else n
    tm = _tile(m, tm_pref, SUBLANES)
    tn = _tile(n, tn_pref, LANES)
    assert col0 % tn == 0
    outs = pl.pallas_call(
        functools.partial(_mm_kernel, out_scale=out_scale),
        grid=(m // tm, n // tn),
        in_specs=[pl.BlockSpec((tm, k), lambda i, j: (i, 0)), _weight_spec(w, w_index, k, tn, col0 // tn)],
        out_specs=[pl.BlockSpec((tm, tn), lambda i, j: (i, j)) for _ in out_dtypes],
        out_shape=[jax.ShapeDtypeStruct((m, n), dt) for dt in out_dtypes],
        compiler_params=_cparams(2),
        name=name,
    )(a, w)
    return outs


def _mm_residual_kernel(a_ref, w_ref, x_ref, gate_ref, o_ref, *, coef):
    o_ref[...] = x_ref[...] + (coef * gate_ref[...]) * _dot(a_ref[...], w_ref[...])


def _mm_residual(a, w, w_index, x, gate, coef, name, tm_pref, tn_pref):
    b, s, n = x.shape
    m, k = a.shape
    tm = _tile(s, tm_pref, SUBLANES)
    tn = _tile(n, tn_pref, LANES)
    tiles_per_batch = s // tm
    out = pl.pallas_call(
        functools.partial(_mm_residual_kernel, coef=coef),
        grid=(m // tm, n // tn),
        in_specs=[pl.BlockSpec((tm, k), lambda i, j: (i, 0)), _weight_spec(w, w_index, k, tn),
                  pl.BlockSpec((tm, tn), lambda i, j: (i, j)),
                  pl.BlockSpec((None, 1, tn), lambda i, j: (i // tiles_per_batch, 0, j))],
        out_specs=pl.BlockSpec((tm, tn), lambda i, j: (i, j)),
        out_shape=jax.ShapeDtypeStruct((m, n), F32),
        compiler_params=_cparams(2),
        name=name,
    )(a, w, x.reshape(m, n), gate)
    return out.reshape(b, s, n)


def _branch(a, w, w_index, x, gate, coef, name, tm_pref, tn_pref):
    b, s, n = x.shape
    if s >= FUSED_RESIDUAL_MIN_ROWS:
        return _mm_residual(a, w, w_index, x, gate, coef, name, tm_pref, tn_pref), None
    (y,) = _mm(a, w, (F32,), name, w_index=w_index, tm_pref=tm_pref, tn_pref=tn_pref)
    return x, (y.reshape(b, s, n), gate, coef)


def _mm_state_kernel(a_ref, w_ref, *refs, n_heads):
    o32_ref, o16_ref = refs[-2:]
    acc = _dot(a_ref[...], w_ref[...])
    o16_ref[...] = acc.astype(BF16)
    hd = acc.shape[1] // n_heads
    for h in range(n_heads):
        o32_ref[:, h, :] = acc[:, h * hd:(h + 1) * hd]


def _mm_state(a, w, w_index, col0, n_heads, layer, stacked, depth, name, tm_pref=512):
    m, k = a.shape
    hd = HEAD_DIM
    n = n_heads * hd
    tm = _tile(m, tm_pref, 2 * SUBLANES)
    assert col0 % n == 0
    in_specs = [pl.BlockSpec((tm, k), lambda i: (i, 0)),
                pl.BlockSpec((None,) * len(w_index) + (k, n), lambda i: tuple(w_index) + (0, col0 // n))]
    args = [a, w]
    aliases = {}
    if stacked is not None:
        in_specs.append(pl.BlockSpec(memory_space=pl.ANY))
        args.append(stacked)
        aliases = {2: 0}
    return pl.pallas_call(
        functools.partial(_mm_state_kernel, n_heads=n_heads),
        grid=(m // tm,),
        in_specs=in_specs,
        out_specs=[pl.BlockSpec((None, tm, n_heads, hd), lambda i: (layer, i, 0, 0)),
                   pl.BlockSpec((tm, n), lambda i: (i, 0))],
        out_shape=[jax.ShapeDtypeStruct((depth, m, n_heads, hd), F32), jax.ShapeDtypeStruct((m, n), BF16)],
        input_output_aliases=aliases,
        compiler_params=_cparams(1),
        name=name,
    )(*args)


def _swiglu_kernel(a_ref, wg_ref, wu_ref, o_ref, *, rows):
    for r0 in range(0, a_ref.shape[0], rows):
        a = a_ref[r0:r0 + rows, :]
        g = _dot(a, wg_ref[...])
        u = _dot(a, wu_ref[...])
        o_ref[r0:r0 + rows, :] = ((g / (1.0 + jnp.exp(-g))) * u).astype(o_ref.dtype)


def _swiglu(a, wg, wu, w_index, tm_pref=2048, tn_pref=256, rows_pref=512):
    m, k = a.shape
    n = wg.shape[-1]
    tm = _tile(m, tm_pref, SUBLANES)
    tn = _tile(n, tn_pref, LANES)
    rows = _tile(tm, rows_pref, SUBLANES)
    wspec = _weight_spec(wg, w_index, k, tn)
    return pl.pallas_call(
        functools.partial(_swiglu_kernel, rows=rows),
        grid=(m // tm, n // tn),
        in_specs=[pl.BlockSpec((tm, k), lambda i, j: (i, 0)), wspec, wspec],
        out_specs=pl.BlockSpec((tm, tn), lambda i, j: (i, j)),
        out_shape=jax.ShapeDtypeStruct((m, n), BF16),
        compiler_params=_cparams(2),
        name="ffn_swiglu",
    )(a, wg, wu)


def _mla_prep_kernel(m_ref, gq_ref, gkv_ref, bf_ref, cos_ref, sin_ref,
                     ql_ref, ckv32_ref, ckv16_ref, kr_ref, *, q_lora, kv_lora, n_forget):
    x = m_ref[...]
    ql_ref[...] = (_rms(x[:, :q_lora]) * gq_ref[...]).astype(BF16)
    ckv = _rms(x[:, q_lora:q_lora + kv_lora]) * gkv_ref[...]
    ckv32_ref[...] = ckv
    ckv16_ref[...] = ckv.astype(BF16)
    t = x[:, q_lora + kv_lora:]
    kr = _rope_tile(t, cos_ref[...], sin_ref[...])
    z = t + bf_ref[...]
    logf = jnp.minimum(z, 0.0) - _softplus_neg_abs(z)
    lane = lax.broadcasted_iota(jnp.int32, t.shape, 1)
    rope_w = LANES // 2
    kr_ref[...] = jnp.where(lane < rope_w, kr, jnp.where(lane < rope_w + n_forget, logf, 0.0))


def _mla_prep(mlat, gq, gkv, bf_tile, cos_t, sin_t, q_lora, kv_lora, n_forget):
    m = mlat.shape[0]
    s = cos_t.shape[0]
    tr = _tile(s, 256, SUBLANES)
    ns = s // tr
    rows = lambda w: pl.BlockSpec((tr, w), lambda i: (i, 0))
    shared = lambda w: pl.BlockSpec((1, w), lambda i: (0, 0))
    table = pl.BlockSpec((tr, LANES), lambda i: (i % ns, 0))
    return pl.pallas_call(
        functools.partial(_mla_prep_kernel, q_lora=q_lora, kv_lora=kv_lora, n_forget=n_forget),
        grid=(m // tr,),
        in_specs=[rows(mlat.shape[1]), shared(q_lora), shared(kv_lora), shared(LANES), table, table],
        out_specs=[rows(q_lora), rows(kv_lora), rows(kv_lora), rows(LANES)],
        out_shape=[
            jax.ShapeDtypeStruct((m, q_lora), BF16),
            jax.ShapeDtypeStruct((m, kv_lora), F32),
            jax.ShapeDtypeStruct((m, kv_lora), BF16),
            jax.ShapeDtypeStruct((m, LANES), F32),
        ],
        compiler_params=_cparams(1),
        name="mla_prep",
    )(mlat, gq.reshape(1, q_lora), gkv.reshape(1, kv_lora), bf_tile, cos_t, sin_t)


def _q_up_kernel(a_ref, w_ref, cos_ref, sin_ref, o_ref, *, n_heads, out_scale):
    acc = _dot(a_ref[...], w_ref[...]) * out_scale
    cos = cos_ref[...]
    sin = sin_ref[...]
    for h in range(n_heads):
        base = h * MLA_QK_PAD
        o_ref[:, base:base + MLA_NOPE] = acc[:, base:base + MLA_NOPE].astype(BF16)
        t = acc[:, base + MLA_NOPE:base + MLA_QK_PAD]
        o_ref[:, base + MLA_NOPE:base + MLA_QK_PAD] = _rope_tile(t, cos, sin).astype(BF16)


def _q_up(qlat, wq, cos_t, sin_t, n_heads, out_scale):
    m, k = qlat.shape
    n = wq.shape[1]
    s = cos_t.shape[0]
    tr = _tile(s, 512, SUBLANES)
    ns = s // tr
    table = pl.BlockSpec((tr, LANES), lambda i: (i % ns, 0))
    return pl.pallas_call(
        functools.partial(_q_up_kernel, n_heads=n_heads, out_scale=out_scale),
        grid=(m // tr,),
        in_specs=[pl.BlockSpec((tr, k), lambda i: (i, 0)), pl.BlockSpec((k, n), lambda i: (0, 0)), table, table],
        out_specs=pl.BlockSpec((tr, n), lambda i: (i, 0)),
        out_shape=jax.ShapeDtypeStruct((m, n), BF16),
        compiler_params=_cparams(1),
        name="mla_q_up",
    )(qlat, wq, cos_t, sin_t)


def _kv_up_kernel(a_ref, wk_ref, wv_ref, kr_ref, k_ref, v_ref, *, n_heads):
    a = a_ref[...].astype(BF16)
    kn = _dot(a, wk_ref[...])
    v_ref[...] = _dot(a, wv_ref[...]).astype(BF16)
    kr = kr_ref[...]
    lane = lax.broadcasted_iota(jnp.int32, kr.shape, 1)
    krm = jnp.where(lane < LANES // 2, kr, 0.0).astype(BF16)
    for h in range(n_heads):
        base = h * MLA_QK_PAD
        k_ref[:, base:base + MLA_NOPE] = kn[:, h * MLA_NOPE:(h + 1) * MLA_NOPE].astype(BF16)
        k_ref[:, base + MLA_NOPE:base + MLA_QK_PAD] = krm


def _kv_up(ckv, wk, wv, kr, n_heads):
    m, k = ckv.shape
    tm = _tile(m, 512, SUBLANES)
    rows = lambda w: pl.BlockSpec((tm, w), lambda i: (i, 0))
    whole = lambda a: pl.BlockSpec(a.shape, lambda i: (0, 0))
    return pl.pallas_call(
        functools.partial(_kv_up_kernel, n_heads=n_heads),
        grid=(m // tm,),
        in_specs=[rows(k), whole(wk), whole(wv), rows(LANES)],
        out_specs=[rows(n_heads * MLA_QK_PAD), rows(n_heads * MLA_V)],
        out_shape=[
            jax.ShapeDtypeStruct((m, n_heads * MLA_QK_PAD), BF16),
            jax.ShapeDtypeStruct((m, n_heads * MLA_V), BF16),
        ],
        compiler_params=_cparams(1),
        name="mla_kv_up",
    )(ckv, wk, wv, kr)


def _split3(x):
    x1 = x.astype(BF16)
    r1 = x - x1.astype(F32)
    x2 = r1.astype(BF16)
    x3 = (r1 - x2.astype(F32)).astype(BF16)
    return x1, x2, x3


def _cumsum_kernel(x_ref, o_ref, *rest, tc, bias_lanes):
    carry_ref = rest[-1]

    @pl.when(pl.program_id(1) == 0)
    def _():
        carry_ref[...] = jnp.zeros_like(carry_ref)

    r = lax.broadcasted_iota(jnp.int32, (tc, tc), 0)
    c = lax.broadcasted_iota(jnp.int32, (tc, tc), 1)
    tri = jnp.where(c <= r, 1.0, 0.0).astype(BF16)
    x1, x2, x3 = _split3(x_ref[0])
    y = (_dot(tri, x3) + _dot(tri, x2)) + _dot(tri, x1) + carry_ref[...]
    o_ref[0] = y
    carry_ref[...] = y[tc - 1:tc, :]
    if bias_lanes is not None:
        kb_ref = rest[0]
        lane0, n_heads = bias_lanes
        parts = [p.astype(F32) for p in _split3(y * (-LOG2E))]
        lane = lax.broadcasted_iota(jnp.int32, (tc, LANES), 1)
        for h in range(n_heads):
            col = [jnp.broadcast_to(p[:, lane0 + h:lane0 + h + 1], (tc, LANES)) for p in parts]
            blk = jnp.where(lane == 0, col[0], jnp.where(lane == 1, col[1], jnp.where(lane == 2, col[2], 0.0)))
            kb_ref[0, :, h * LANES:(h + 1) * LANES] = blk.astype(BF16)


def _cumsum_rows(x, bias_lanes=None):
    b, s, w = x.shape
    tc = _tile(s, 256, 2 * SUBLANES)
    blk = pl.BlockSpec((1, tc, w), lambda bi, j: (bi, j, 0))
    out_specs, out_shape = [blk], [jax.ShapeDtypeStruct((b, s, w), F32)]
    if bias_lanes is not None:
        wide = bias_lanes[1] * LANES
        out_specs.append(pl.BlockSpec((1, tc, wide), lambda bi, j: (bi, j, 0)))
        out_shape.append(jax.ShapeDtypeStruct((b, s, wide), BF16))
    outs = pl.pallas_call(
        functools.partial(_cumsum_kernel, tc=tc, bias_lanes=bias_lanes),
        grid=(b, s // tc),
        in_specs=[blk],
        out_specs=out_specs,
        out_shape=out_shape,
        scratch_shapes=[pltpu.VMEM((1, w), F32)],
        compiler_params=_cparams(2, sequential_axes=(1,)),
        name="cumsum_rows",
    )(x)
    return outs if bias_lanes is not None else outs[0]


def _rep_lanes(x, width):
    return x if width == LANES else jnp.concatenate([x] * (width // LANES), axis=1)


def _softmax_attn_kernel(*refs, t, group, dq, dv, chunk, has_bias):
    if has_bias:
        qt_ref, k_ref, kb_ref, vt_ref, o_ref, m_ref, l_ref, acc_ref, sa_ref, sb_ref = refs
    else:
        qt_ref, k_ref, vt_ref, o_ref, m_ref, l_ref, acc_ref, sa_ref, sb_ref = refs
    i = pl.program_id(2)
    m_ref[...] = jnp.full_like(m_ref, -jnp.inf)
    l_ref[...] = jnp.zeros_like(l_ref)
    acc_ref[...] = jnp.zeros_like(acc_ref)
    if has_bias:
        row = lax.broadcasted_iota(jnp.int32, (LANES, t), 0)
        ones_rows = jnp.where(row < N_BIAS_TERMS, 1.0, 0.0).astype(BF16)

    def score(g, ks):
        k = k_ref[0, pl.ds(ks, t), g * dq:(g + 1) * dq]
        qt = qt_ref[0, g * dq:(g + 1) * dq, :]
        if has_bias:
            k = jnp.concatenate([k, kb_ref[0, pl.ds(ks, t), g * LANES:(g + 1) * LANES]], axis=1)
            qt = jnp.concatenate([qt, ones_rows], axis=0)
        return _dot(k, qt)

    heads = range(group)

    def scores_into(j, s_ref):
        ks = pl.multiple_of(j * t, t)
        for g in heads:
            s_ref[g] = score(g, ks)

    def fold(j, s_ref, masked):
        ks = pl.multiple_of(j * t, t)
        probs, alphas = [], []
        for g in heads:
            s = s_ref[g]
            if masked:
                kpos = lax.broadcasted_iota(jnp.int32, s.shape, 0)
                qpos = lax.broadcasted_iota(jnp.int32, s.shape, 1)
                visible = (kpos // chunk) <= (qpos // chunk) if chunk else kpos <= qpos
                s = jnp.where(visible, s, -jnp.inf)
            m_prev = m_ref[g]
            m_new = jnp.maximum(m_prev, jnp.max(s, axis=0, keepdims=True))
            alpha = jnp.exp2(m_prev - m_new)
            p = jnp.exp2(s - m_new)
            l_ref[g] = alpha * l_ref[g] + jnp.sum(p, axis=0, keepdims=True)
            m_ref[g] = m_new
            probs.append(p.astype(BF16))
            alphas.append(alpha)
        for g in heads:
            pv = _dot(vt_ref[0, g * dv:(g + 1) * dv, pl.ds(ks, t)], probs[g])
            acc_ref[g] = alphas[g] * acc_ref[g] + pv

    scores_into(0, sa_ref)

    def body(jj, carry):
        j = 2 * jj
        scores_into(j + 1, sb_ref)
        fold(j, sa_ref, False)
        scores_into(j + 2, sa_ref)
        fold(j + 1, sb_ref, False)
        return carry

    lax.fori_loop(0, i // 2, body, 0)

    @pl.when(i % 2 == 0)
    def _():
        fold(i, sa_ref, True)

    @pl.when(i % 2 == 1)
    def _():
        scores_into(i, sb_ref)
        fold(i - 1, sa_ref, False)
        fold(i, sb_ref, True)

    for g in range(group):
        o_ref[0, :, g * dv:(g + 1) * dv] = (acc_ref[g] / l_ref[g]).T


def _head_group(n_heads):
    return ATTN_HEAD_GROUP if n_heads % ATTN_HEAD_GROUP == 0 else 1


def _softmax_attn(qt, k, vt, n_heads, dq, dv, chunk, name, kb=None):
    b, s, _ = k.shape
    t = _tile(s, ATTN_TILE, LANES)
    grp = _head_group(n_heads)
    stat = pltpu.VMEM((grp, 1, t), F32)
    keys = lambda w: pl.BlockSpec((1, s, grp * w), lambda bi, h, i: (bi, 0, h))
    in_specs = [pl.BlockSpec((1, grp * dq, t), lambda bi, h, i: (bi, h, i)), keys(dq)]
    args = [qt, k]
    if kb is not None:
        in_specs.append(keys(LANES))
        args.append(kb)
    in_specs.append(pl.BlockSpec((1, grp * dv, s), lambda bi, h, i: (bi, h, 0)))
    args.append(vt)
    return pl.pallas_call(
        functools.partial(_softmax_attn_kernel, t=t, group=grp, dq=dq, dv=dv, chunk=chunk, has_bias=kb is not None),
        grid=(b, n_heads // grp, s // t),
        in_specs=in_specs,
        out_specs=pl.BlockSpec((1, t, grp * dv), lambda bi, h, i: (bi, i, h)),
        out_shape=jax.ShapeDtypeStruct((b, s, n_heads * dv), F32),
        scratch_shapes=[stat, stat, pltpu.VMEM((grp, dv, t), F32),
                        pltpu.VMEM((grp, t, t), F32), pltpu.VMEM((grp, t, t), F32)],
        compiler_params=_cparams(3),
        name=name,
    )(*args)


def _softmax_attn_cached_kernel(*refs, has_bias, chunk, past):
    if has_bias:
        q_ref, kc_ref, vc_ref, kn_ref, vn_ref, fkc_ref, fkn_ref, o_ref = refs
    else:
        q_ref, kc_ref, vc_ref, kn_ref, vn_ref, o_ref = refs
    q = q_ref[...]
    s_c = _dot_nt(q, kc_ref[...].astype(BF16))
    s_n = _dot_nt(q, kn_ref[...])
    if has_bias:
        s_c = s_c - fkc_ref[...] * LOG2E
        s_n = s_n - fkn_ref[...] * LOG2E
    r = lax.broadcasted_iota(jnp.int32, s_n.shape, 0) + past
    c = lax.broadcasted_iota(jnp.int32, s_n.shape, 1) + past
    visible = (c // chunk) <= (r // chunk) if chunk else c <= r
    s_n = jnp.where(visible, s_n, -jnp.inf)
    m = jnp.maximum(jnp.max(s_c, axis=-1, keepdims=True), jnp.max(s_n, axis=-1, keepdims=True))
    p_c = jnp.exp2(s_c - m)
    p_n = jnp.exp2(s_n - m)
    l = jnp.sum(p_c, axis=-1, keepdims=True) + jnp.sum(p_n, axis=-1, keepdims=True)
    acc = _dot(p_c.astype(BF16), vc_ref[...].astype(BF16)) + _dot(p_n.astype(BF16), vn_ref[...])
    o_ref[...] = acc / l


def _cache_spec(cache, layer, d):
    if layer is None:
        return pl.BlockSpec((None, cache.shape[1], d), lambda bi, h: (bi, 0, h))
    return pl.BlockSpec((None, None, cache.shape[2], d), lambda bi, h: (layer, bi, 0, h))


def _softmax_attn_cached(q, kc, vc, kn, vn, n_heads, dq, dv, chunk, f_row_c=None, f_row_n=None, layer=None):
    b, sn, _ = q.shape
    past = kc.shape[-2]
    has_bias = f_row_c is not None
    new = lambda d: pl.BlockSpec((None, sn, d), lambda bi, h: (bi, 0, h))
    in_specs = [new(dq), _cache_spec(kc, layer, dq), _cache_spec(vc, layer, dv), new(dq), new(dv)]
    args = [q, kc, vc, kn, vn]
    if has_bias:
        in_specs += [
            pl.BlockSpec((None, None, 1, past), lambda bi, h: (bi, h, 0, 0)),
            pl.BlockSpec((None, None, 1, sn), lambda bi, h: (bi, h, 0, 0)),
        ]
        args += [f_row_c, f_row_n]
    return pl.pallas_call(
        functools.partial(_softmax_attn_cached_kernel, has_bias=has_bias, chunk=chunk, past=past),
        grid=(b, n_heads),
        in_specs=in_specs,
        out_specs=new(dv),
        out_shape=jax.ShapeDtypeStruct((b, sn, n_heads * dv), F32),
        compiler_params=_cparams(2),
        name="fox_attn_cached" if has_bias else "mla_attn_cached",
    )(*args)


def _cache_block(layer, tp, n_heads, d, order):
    return pl.BlockSpec((None, None, tp, n_heads, d), lambda bi, j: (layer, bi, order(j), 0, 0))


def _fox_cached_kernel(q_ref, kc_ref, vc_ref, kn_ref, vn_ref, fkc_ref, fkn_ref, o_ref, m_ref, l_ref, acc_ref,
                       *, n_heads, d):
    j = pl.program_id(1)

    def update(h, s, v):
        m_prev = m_ref[h]
        m_new = jnp.maximum(m_prev, jnp.max(s, axis=-1, keepdims=True))
        alpha = jnp.exp2(m_prev - m_new)
        p = jnp.exp2(s - m_new)
        l_ref[h] = alpha * l_ref[h] + jnp.sum(p, axis=-1, keepdims=True)
        acc_ref[h] = alpha * acc_ref[h] + _dot(p.astype(BF16), v)
        m_ref[h] = m_new

    @pl.when(j == 0)
    def _():
        m_ref[...] = jnp.full_like(m_ref, -jnp.inf)
        l_ref[...] = jnp.zeros_like(l_ref)
        acc_ref[...] = jnp.zeros_like(acc_ref)
        for h in range(n_heads):
            cols = slice(h * d, (h + 1) * d)
            s = _dot_nt(q_ref[:, cols], kn_ref[:, cols]) - fkn_ref[h] * LOG2E
            r = lax.broadcasted_iota(jnp.int32, s.shape, 0)
            c = lax.broadcasted_iota(jnp.int32, s.shape, 1)
            update(h, jnp.where(c <= r, s, -jnp.inf), vn_ref[:, cols])

    for h in range(n_heads):
        s = _dot_nt(q_ref[:, h * d:(h + 1) * d], kc_ref[:, h, :].astype(BF16)) - fkc_ref[h] * LOG2E
        update(h, s, vc_ref[:, h, :].astype(BF16))

    @pl.when(j == pl.num_programs(1) - 1)
    def _():
        for h in range(n_heads):
            o_ref[:, h * d:(h + 1) * d] = acc_ref[h] / l_ref[h]


def _fox_attn_cached(q, kc, vc, kn, vn, f_row_c, f_row_n, layer):
    b, sn, _ = q.shape
    _, _, past, n_heads, d = kc.shape
    tp = _tile(past, CACHE_BLOCK, LANES)
    new = pl.BlockSpec((None, sn, n_heads * d), lambda bi, j: (bi, 0, 0))
    cache = _cache_block(layer, tp, n_heads, d, lambda j: j)
    stat = pltpu.VMEM((n_heads, sn, 1), F32)
    return pl.pallas_call(
        functools.partial(_fox_cached_kernel, n_heads=n_heads, d=d),
        grid=(b, past // tp),
        in_specs=[new, cache, cache, new, new,
                  pl.BlockSpec((None, n_heads, 1, tp), lambda bi, j: (bi, 0, 0, j)),
                  pl.BlockSpec((None, n_heads, 1, sn), lambda bi, j: (bi, 0, 0, 0))],
        out_specs=new,
        out_shape=jax.ShapeDtypeStruct((b, sn, n_heads * d), F32),
        scratch_shapes=[stat, stat, pltpu.VMEM((n_heads, sn, d), F32)],
        compiler_params=_cparams(2, sequential_axes=(1,)),
        name="fox_attn_cached",
    )(q, kc, vc, kn, vn, f_row_c, f_row_n)


def _suffix_matrix(t):
    r = lax.broadcasted_iota(jnp.int32, (t, t), 0)
    c = lax.broadcasted_iota(jnp.int32, (t, t), 1)
    return jnp.where(r >= c, 1.0, 0.0).astype(BF16)


def _sb_fold(q, k, v, u, c, strict):
    z = _dot_nt(q, k)
    neg_abs = lax.bitcast_convert_type(lax.bitcast_convert_type(z, jnp.uint32) | jnp.uint32(0x80000000), F32)
    sp = jnp.maximum(z, 0.0) + jnp.log(1.0 + jnp.exp2(neg_abs)) * LOG2E
    if strict is not None:
        sp = jnp.where(strict, sp, 0.0)
    hi = lax.bitcast_convert_type(lax.bitcast_convert_type(sp, jnp.uint32) & jnp.uint32(0xFFFF0000), F32)
    lo = (sp - hi).astype(BF16)
    suffix = _dot(lo, u) + _dot(hi.astype(BF16), u)
    width = z.shape[1]
    c_wide = c[:, :width] if width < LANES else _rep_lanes(c, width)
    a = jnp.exp2(z - (suffix + c_wide))
    if strict is not None:
        a = jnp.where(strict, a, 0.0)
    return _dot(a.astype(BF16), v), c + jnp.sum(sp, axis=-1, keepdims=True)


def _sb_attn_kernel(qt_ref, k_ref, vt_ref, o_ref, c_ref, acc_ref, *bufs, t, tsub, group, d):
    i = pl.program_id(2)
    buf_a, buf_b = bufs[:3], bufs[3:]
    r = lax.broadcasted_iota(jnp.int32, (tsub, tsub), 0)
    col = lax.broadcasted_iota(jnp.int32, (tsub, tsub), 1)
    ut = jnp.where(col >= r, 1.0, 0.0).astype(BF16)
    c_ref[...] = jnp.zeros_like(c_ref)
    acc_ref[...] = jnp.zeros_like(acc_ref)
    n_sub = t // tsub
    chains = [(g, sub) for g in range(group) for sub in reversed(range(n_sub))]

    def strict_mask(sub):
        kpos = lax.broadcasted_iota(jnp.int32, (tsub, t), 0) + sub * tsub
        qpos = lax.broadcasted_iota(jnp.int32, (tsub, t), 1)
        return kpos < qpos

    def scores(j, buf, masked):
        z_ref, hi_ref, lo_ref = buf
        z = {}
        for g, sub in chains:
            start = pl.multiple_of(j * t + sub * tsub, tsub)
            z[g, sub] = _dot(k_ref[0, pl.ds(start, tsub), g * d:(g + 1) * d], qt_ref[0, g * d:(g + 1) * d, :])
        for g, sub in chains:
            zc = z[g, sub]
            neg_abs = lax.bitcast_convert_type(lax.bitcast_convert_type(zc, jnp.uint32) | jnp.uint32(0x80000000), F32)
            sp = jnp.maximum(zc, 0.0) + jnp.log(1.0 + jnp.exp2(neg_abs)) * LOG2E
            if masked:
                sp = jnp.where(strict_mask(sub), sp, 0.0)
            hi = lax.bitcast_convert_type(lax.bitcast_convert_type(sp, jnp.uint32) & jnp.uint32(0xFFFF0000), F32)
            z_ref[g, sub] = zc
            hi_ref[g, sub] = hi.astype(BF16)
            lo_ref[g, sub] = (sp - hi).astype(BF16)

    def fold(j, buf, masked):
        z_ref, hi_ref, lo_ref = buf
        suffix = {(g, sub): _dot(ut, lo_ref[g, sub]) + _dot(ut, hi_ref[g, sub]) for g, sub in chains}
        c = {g: c_ref[g] for g in range(group)}
        acc = {g: acc_ref[g] for g in range(group)}
        for g, sub in chains:
            start = pl.multiple_of(j * t + sub * tsub, tsub)
            a = jnp.exp2(z_ref[g, sub] - (suffix[g, sub] + c[g]))
            if masked:
                a = jnp.where(strict_mask(sub), a, 0.0)
            acc[g] = acc[g] + _dot(vt_ref[0, g * d:(g + 1) * d, pl.ds(start, tsub)], a.astype(BF16))
            c[g] = c[g] + suffix[g, sub][0:1, :]
        for g in range(group):
            c_ref[g] = c[g]
            acc_ref[g] = acc[g]

    scores(i, buf_a, True)
    fold(i, buf_a, True)
    block = lambda n: jnp.maximum(i - 1 - n, 0)
    scores(block(0), buf_a, False)

    def body(jj, carry):
        n = 2 * jj
        scores(block(n + 1), buf_b, False)
        fold(block(n), buf_a, False)
        scores(block(n + 2), buf_a, False)
        fold(block(n + 1), buf_b, False)
        return carry

    lax.fori_loop(0, i // 2, body, 0)

    @pl.when(i % 2 == 1)
    def _():
        fold(0, buf_a, False)

    for g in range(group):
        o_ref[0, :, g * d:(g + 1) * d] = acc_ref[g].T


def _sb_attn(qt, k, vt, n_heads):
    b, s, _ = k.shape
    d = HEAD_DIM
    t = _tile(s, ATTN_TILE, LANES)
    tsub = _tile(t, SB_SUB_TILE, LANES)
    grp = _head_group(n_heads)
    return pl.pallas_call(
        functools.partial(_sb_attn_kernel, t=t, tsub=tsub, group=grp, d=d),
        grid=(b, n_heads // grp, s // t),
        in_specs=[
            pl.BlockSpec((1, grp * d, t), lambda bi, h, i: (bi, h, i)),
            pl.BlockSpec((1, s, grp * d), lambda bi, h, i: (bi, 0, h)),
            pl.BlockSpec((1, grp * d, s), lambda bi, h, i: (bi, h, 0)),
        ],
        out_specs=pl.BlockSpec((1, t, grp * d), lambda bi, h, i: (bi, i, h)),
        out_shape=jax.ShapeDtypeStruct((b, s, n_heads * d), F32),
        scratch_shapes=[pltpu.VMEM((grp, 1, t), F32), pltpu.VMEM((grp, d, t), F32)]
        + [pltpu.VMEM((grp, t // tsub, tsub, t), dt) for dt in (F32, BF16, BF16)] * 2,
        compiler_params=_cparams(3),
        name="sb_attn",
    )(qt, k, vt)


def _sb_cached_kernel(q_ref, kc_ref, vc_ref, kn_ref, vn_ref, o_ref, c_ref, acc_ref, *, n_heads, d, tk):
    j = pl.program_id(1)
    sn = q_ref.shape[0]

    @pl.when(j == 0)
    def _():
        r = lax.broadcasted_iota(jnp.int32, (sn, sn), 0)
        col = lax.broadcasted_iota(jnp.int32, (sn, sn), 1)
        u_new = _suffix_matrix(sn)
        for h in range(n_heads):
            cols = slice(h * d, (h + 1) * d)
            acc_ref[h], c_ref[h] = _sb_fold(q_ref[:, cols], kn_ref[:, cols], vn_ref[:, cols], u_new,
                                            jnp.zeros((sn, LANES), F32), col < r)

    u = _suffix_matrix(tk)
    for h in range(n_heads):
        q = q_ref[:, h * d:(h + 1) * d]
        c = c_ref[h]
        acc = acc_ref[h]
        for blk in reversed(range(kc_ref.shape[0] // tk)):
            rows = slice(blk * tk, (blk + 1) * tk)
            pv, c = _sb_fold(q, kc_ref[rows, h, :].astype(BF16), vc_ref[rows, h, :].astype(BF16), u, c, None)
            acc = acc + pv
        c_ref[h] = c
        acc_ref[h] = acc

    @pl.when(j == pl.num_programs(1) - 1)
    def _():
        for h in range(n_heads):
            o_ref[:, h * d:(h + 1) * d] = acc_ref[h]


def _sb_attn_cached(q, kc, vc, kn, vn, layer):
    b, sn, _ = q.shape
    _, _, past, n_heads, d = kc.shape
    tp = _tile(past, CACHE_BLOCK, LANES)
    tk = _tile(tp, SB_SUB_TILE, LANES)
    n_blocks = past // tp
    new = pl.BlockSpec((None, sn, n_heads * d), lambda bi, j: (bi, 0, 0))
    cache = _cache_block(layer, tp, n_heads, d, lambda j: n_blocks - 1 - j)
    return pl.pallas_call(
        functools.partial(_sb_cached_kernel, n_heads=n_heads, d=d, tk=tk),
        grid=(b, n_blocks),
        in_specs=[new, cache, cache, new, new],
        out_specs=new,
        out_shape=jax.ShapeDtypeStruct((b, sn, n_heads * d), F32),
        scratch_shapes=[pltpu.VMEM((n_heads, sn, LANES), F32), pltpu.VMEM((n_heads, sn, d), F32)],
        compiler_params=_cparams(2, sequential_axes=(1,)),
        name="sb_attn_cached",
    )(q, kc, vc, kn, vn)


def _group_norm_kernel(of_ref, os_ref, om_ref, g_ref, h_ref):
    off = 0
    for ref in (of_ref, os_ref, om_ref):
        w = ref.shape[-1]
        h_ref[0, :, off:off + w] = (_rms(ref[0]) * g_ref[:, off:off + w]).astype(BF16)
        off += w


def _group_norm(o_fox, o_sb, o_mla, g):
    b, s, _ = o_fox.shape
    d = g.shape[-1]
    tr = _tile(s, 256, SUBLANES)
    rows = lambda a: pl.BlockSpec((1, tr, a.shape[-1]), lambda bi, i: (bi, i, 0))
    return pl.pallas_call(
        _group_norm_kernel,
        grid=(b, s // tr),
        in_specs=[rows(o_fox), rows(o_sb), rows(o_mla), pl.BlockSpec((1, d), lambda bi, i: (0, 0))],
        out_specs=pl.BlockSpec((1, tr, d), lambda bi, i: (bi, i, 0)),
        out_shape=jax.ShapeDtypeStruct((b, s, d), BF16),
        compiler_params=_cparams(2),
        name="group_norm",
    )(o_fox, o_sb, o_mla, g.reshape(1, d))


def _rope_tables(pos, rope_dim):
    half = rope_dim // 2
    inv = ROPE_THETA ** (-jnp.arange(half, dtype=F32) / half)
    ang = pos.astype(F32)[:, None] * inv[None, :]
    cos, sin = jnp.cos(ang), jnp.sin(ang)
    n = pos.shape[0]
    pad = LANES - rope_dim
    cos_t = jnp.concatenate([cos, cos, jnp.ones((n, pad), F32)], axis=-1)
    sin_t = jnp.concatenate([-sin, sin, jnp.zeros((n, pad), F32)], axis=-1)
    return cos_t, sin_t


def _prep_layer_weights(l, dims, w_in, b_forget, w_q_up, w_kv_up):
    hf, hs, hm, q_lora, kv_lora, rope = dims
    wf, ws = hf * HEAD_DIM, hs * HEAD_DIM
    d = w_in.shape[1]
    w = w_in[l]
    widths = (wf, wf, wf, hf, ws, ws, ws, q_lora, kv_lora, rope)
    offs = [0]
    for x in widths:
        offs.append(offs[-1] + x)
    col = lambda i: w[:, offs[i]:offs[i + 1]]
    lw = {"fox_cols": (offs[0], offs[1], offs[2])}
    for name, i in (("sq", 4), ("sk", 5), ("sv", 6)):
        lw[name] = col(i).astype(BF16)
    pad = jnp.zeros((d, LANES - rope - hf), F32)
    lw["mla_in"] = jnp.concatenate([col(7), col(8), col(9), col(3), pad], axis=1).astype(BF16)
    lw["bf"] = jnp.zeros((1, LANES), F32).at[0, rope:rope + hf].set(b_forget[l])
    wq = w_q_up[l].reshape(q_lora, hm, MLA_NOPE + rope)
    wq = jnp.pad(wq, ((0, 0), (0, 0), (0, MLA_QK_PAD - MLA_NOPE - rope)))
    lw["wq"] = wq.reshape(q_lora, hm * MLA_QK_PAD).astype(BF16)
    wkv = w_kv_up[l].reshape(kv_lora, hm, MLA_NOPE + MLA_V)
    lw["wk"] = wkv[:, :, :MLA_NOPE].reshape(kv_lora, hm * MLA_NOPE).astype(BF16)
    lw["wv"] = wkv[:, :, MLA_NOPE:].reshape(kv_lora, hm * MLA_V).astype(BF16)
    return lw


def _ffn(h, gw, l, f, x, gate, coef):
    b, s, d = h.shape
    a = _swiglu(h.reshape(b * s, d), gw["wg"], gw["wu"], (l, f))
    return _branch(a, gw["wd"], (l, f), x, gate, coef, "ffn_down", 512, 512)


def _token_mix(h, l, lw, gw, dims, tables, cache, stacked, x, gate):
    hf, hs, hm, q_lora, kv_lora, rope = dims
    b, s, d = h.shape
    h2 = h.reshape(b * s, d)
    cos_t, sin_t = tables
    three = lambda a: a.reshape(b, s, -1)
    depth = gw["w_in"].shape[0]

    qk_scale = HEAD_DIM ** -0.5 * LOG2E
    mla_scale = (MLA_NOPE + rope) ** -0.5 * LOG2E
    fq_col, fk_col, fv_col = lw["fox_cols"]
    wf = hf * HEAD_DIM
    stacked = dict(stacked)
    (fq,) = _mm(h2, gw["w_in"], (BF16,), "in_fox_q", w_index=(l,), col0=fq_col, n=wf, out_scale=qk_scale)
    stacked["fk"], fk16 = _mm_state(h2, gw["w_in"], (l,), fk_col, hf, l, stacked["fk"], depth, "in_fox_k")
    stacked["fv"], fv16 = _mm_state(h2, gw["w_in"], (l,), fv_col, hf, l, stacked["fv"], depth, "in_fox_v")
    (sq,) = _mm(h2, lw["sq"], (BF16,), "in_sb_q", out_scale=qk_scale)
    stacked["sk"], sk16 = _mm_state(h2, lw["sk"], (), 0, hs, l, stacked["sk"], depth, "in_sb_k")
    stacked["sv"], sv16 = _mm_state(h2, lw["sv"], (), 0, hs, l, stacked["sv"], depth, "in_sb_v")
    (mlat,) = _mm(h2, lw["mla_in"], (F32,), "in_mla", tm_pref=512, tn_pref=lw["mla_in"].shape[1])

    qlat, ckv32, ckv16, krlf = _mla_prep(mlat, lw["gq"], lw["gkv"], lw["bf"], cos_t, sin_t, q_lora, kv_lora, hf)
    q_mla = _q_up(qlat, lw["wq"], cos_t, sin_t, hm, mla_scale)
    k_mla, v_mla = _kv_up(ckv16, lw["wk"], lw["wv"], krlf, hm)

    krlf3 = three(krlf)
    krope = krlf3[:, :, :rope]
    logf = krlf3[:, :, rope:rope + hf]

    if cache is None:
        transposed = lambda a: jnp.swapaxes(three(a), 1, 2)
        _, k_bias = _cumsum_rows(krlf3, bias_lanes=(rope, hf))
        o_fox = _softmax_attn(transposed(fq), three(fk16), transposed(fv16), hf, HEAD_DIM, HEAD_DIM, 0,
                              "fox_attn", kb=k_bias)
        o_sb = _sb_attn(transposed(sq), three(sk16), transposed(sv16), hs)
        o_mla = _softmax_attn(transposed(q_mla), three(k_mla), transposed(v_mla), hm, MLA_QK_PAD, MLA_V, CHUNK, "mla_attn")
    else:
        c_fk, c_fv, c_logf, c_sk, c_sv, c_ckv, c_krope = cache
        past = c_fk.shape[2]
        flat = lambda a: a.reshape(b, past, -1)
        logf_all = jnp.concatenate([c_logf[l].astype(F32), logf], axis=1)
        logf_all = jnp.pad(logf_all, ((0, 0), (0, 0), (0, LANES - hf)))
        f_all = _cumsum_rows(logf_all)[:, :, :hf]
        f_bh = jnp.swapaxes(f_all, 1, 2)
        o_fox = _fox_attn_cached(three(fq), c_fk, c_fv, three(fk16), three(fv16),
                                 f_bh[:, :, None, :past], f_bh[:, :, None, past:], l)
        o_sb = _sb_attn_cached(three(sq), c_sk, c_sv, three(sk16), three(sv16), l)
        c_kr = jnp.pad(c_krope[l].reshape(b * past, rope), ((0, 0), (0, LANES - rope)))
        kc_mla, vc_mla = _kv_up(c_ckv[l].reshape(b * past, kv_lora), lw["wk"], lw["wv"], c_kr, hm)
        o_mla = _softmax_attn_cached(
            three(q_mla), flat(kc_mla), flat(vc_mla), three(k_mla), three(v_mla), hm, MLA_QK_PAD, MLA_V, CHUNK)

    hmix = _group_norm(o_fox, o_sb, o_mla, lw["g_mix"])
    mixed = _branch(hmix.reshape(b * s, d), gw["w_out"], (l,), x, gate, 1.0, "mix_out", 1024, 512)
    return mixed, (logf, ckv32.reshape(b, s, kv_lora), krope), stacked


def kernel(x_prompt, x_sample, c_prompt, c_sample, cache_fox_k, cache_fox_v, cache_fox_logf, cache_sb_k, cache_sb_v, cache_mla_ckv, cache_mla_krope, w_ada, b_ada, g_norm, w_in, b_forget, g_q_lat, w_q_up, g_kv_lat, w_kv_up, g_mix_out, w_out, w_ffn_gate, w_ffn_up, w_ffn_down, g_final):
    depth = w_ada.shape[0]
    d = x_prompt.shape[-1]
    nbp, nbs = x_prompt.shape[0], x_sample.shape[0]
    past = cache_fox_k.shape[2]
    hf, hs = cache_fox_k.shape[3], cache_sb_k.shape[3]
    kv_lora, rope, q_lora = cache_mla_ckv.shape[-1], cache_mla_krope.shape[-1], g_q_lat.shape[-1]
    hm = w_q_up.shape[-1] // (MLA_NOPE + rope)
    dims = (hf, hs, hm, q_lora, kv_lora, rope)

    c_all = jnp.concatenate([c_prompt, c_sample], axis=0)
    nb = nbp + nbs
    nb_pad = -(-nb // SUBLANES) * SUBLANES
    c_all = jnp.pad(c_all, ((0, nb_pad - nb), (0, 0)))
    mods = _ada(c_all, w_ada, b_ada).reshape(depth, nb_pad, N_MOD, 1, d)
    groups = {"p": (0, nbp), "s": (nbp, nb)}
    mod = lambda l, grp, i: mods[l, groups[grp][0]:groups[grp][1], i]

    tables = {
        "p": _rope_tables(jnp.arange(x_prompt.shape[1]), rope),
        "s": _rope_tables(past + jnp.arange(x_sample.shape[1]), rope),
    }
    caches = (cache_fox_k, cache_fox_v, cache_fox_logf, cache_sb_k, cache_sb_v, cache_mla_ckv, cache_mla_krope)

    gw = {"w_in": w_in.astype(BF16), "w_out": w_out.astype(BF16), "wg": w_ffn_gate.astype(BF16),
          "wu": w_ffn_up.astype(BF16), "wd": w_ffn_down.astype(BF16)}

    xs = {"p": x_prompt, "s": x_sample}
    pending = {"p": None, "s": None}
    small = {"p": [], "s": []}
    stacked = {grp: {"fk": None, "fv": None, "sk": None, "sv": None} for grp in ("p", "s")}
    for l in range(depth):
        lw = _prep_layer_weights(l, dims, w_in, b_forget, w_q_up, w_kv_up)
        lw["gq"], lw["gkv"], lw["g_mix"] = g_q_lat[l], g_kv_lat[l], g_mix_out[l]
        for grp in ("p", "s"):
            m = lambda i: mod(l, grp, i)
            x, h = _prenorm(xs[grp], g_norm[l, 0], m(1), m(0), res=pending[grp])
            x, pend = _ffn(h, gw, l, 0, x, m(2), 0.5)
            x, h = _prenorm(x, g_norm[l, 1], m(4), m(3), res=pend)
            cache = None if grp == "p" else caches
            (x, pend), st, stacked[grp] = _token_mix(h, l, lw, gw, dims, tables[grp], cache, stacked[grp], x, m(5))
            small[grp].append(st)
            x, h = _prenorm(x, g_norm[l, 2], m(7), m(6), res=pend)
            xs[grp], pending[grp] = _ffn(h, gw, l, 1, x, m(8), 0.5)

    y_prompt = _prenorm(xs["p"], g_final, res=pending["p"], final=True)
    y_sample = _prenorm(xs["s"], g_final, res=pending["s"], final=True)

    def states(grp, x):
        bsz, seq = x.shape[0], x.shape[1]
        big = lambda name: stacked[grp][name].reshape((depth, bsz, seq) + stacked[grp][name].shape[2:])
        stack = lambda i: jnp.stack([st[i] for st in small[grp]])
        return (big("fk"), big("fv"), stack(0), big("sk"), big("sv"), stack(1), stack(2))

    return (y_prompt, y_sample) + states("p", x_prompt) + states("s", x_sample)
```
